```python
import math
import jax, jax.numpy as jnp
from jax import lax
import numpy as np

D_MODEL = 1024
BATCH = 8
SEQ = 8192
DEPTH = 4

HEAD_DIM = 64
N_HEADS_A = 8
ROT_DIM_A = HEAD_DIM // 4
DILATED_BRANCHES = ((128, 1), (512, 4), (2048, 16))
FNET_GROUPS = 4
FNET_GROUP_DIM = 64
N_HEADS_C = 8
C_NOPE_DIM = 64
C_ROPE_DIM = 32
C_V_DIM = 64
C_Q_RANK = 256
C_KV_RANK = 128
S5_GROUPS = 16
S5_GROUP_DIM = 16
S5_STATE = 64
S5_MIN_STEP = 1e-3
S5_MAX_STEP = 1e-1
D_FF = 2816
ROPE_THETA = 500000.0
Q_BLOCK = 128
NORM_EPS = 1e-6
NEG_INF = -1e30
MAX_POS_OFFSET = 1024

A_WIDTH = N_HEADS_A * HEAD_DIM
B_WIDTH = FNET_GROUPS * FNET_GROUP_DIM
EVEN_IN = 3 * A_WIDTH + B_WIDTH
EVEN_OUT = A_WIDTH + B_WIDTH
C_QK_DIM = C_NOPE_DIM + C_ROPE_DIM
C_WIDTH = N_HEADS_C * C_V_DIM
D_WIDTH = S5_GROUPS * S5_GROUP_DIM
ODD_IN = C_Q_RANK + C_KV_RANK + C_ROPE_DIM + D_WIDTH
ODD_OUT = C_WIDTH + D_WIDTH

kernel_name = 'hybrid_dilated_fnet_mla_s5_encoder'


def rms_norm(x, g):
    xf = x.astype(jnp.float32)
    y = xf * lax.rsqrt(jnp.mean(xf * xf, axis=-1, keepdims=True) + NORM_EPS)
    return (y * g.astype(jnp.float32)).astype(x.dtype)


def swiglu(h, w_gate, w_up, w_down):
    return (jax.nn.silu(h @ w_gate) * (h @ w_up)) @ w_down


def rotary_tables(positions, rot_dim):
    inv_freq = ROPE_THETA ** (-jnp.arange(0, rot_dim, 2, dtype=jnp.float32) / rot_dim)
    ang = positions.astype(jnp.float32)[..., None] * inv_freq
    ang = jnp.concatenate([ang, ang], axis=-1)
    return jnp.cos(ang)[:, :, None, :], jnp.sin(ang)[:, :, None, :]


def apply_rotary(t, cos, sin):
    tf = t.astype(jnp.float32)
    half = t.shape[-1] // 2
    rot = jnp.concatenate([-tf[..., half:], tf[..., :half]], axis=-1)
    return (tf * cos + rot * sin).astype(t.dtype)


def dilated_window_branch(q, k, v, dilation, radius):
    bsz, s_len, n_h, e = q.shape
    L = s_len // dilation
    nb = -(-L // radius)
    lp = nb * radius
    pad = lp - L

    def to_sub(t):
        return t.reshape(bsz, L, dilation, n_h, e).transpose(0, 2, 1, 3, 4)

    qs = jnp.pad(to_sub(q), ((0, 0), (0, 0), (0, pad), (0, 0), (0, 0)))
    qs = qs.reshape(bsz, dilation, nb, radius, n_h, e)

    def neighbourhood(t):
        t = jnp.pad(to_sub(t), ((0, 0), (0, 0), (radius, pad + radius), (0, 0), (0, 0)))
        t = t.reshape(bsz, dilation, nb + 2, radius, n_h, e)
        return jnp.concatenate([t[:, :, :-2], t[:, :, 1:-1], t[:, :, 2:]], axis=3)

    kn = neighbourhood(k).astype(jnp.float32)
    vn = neighbourhood(v).astype(jnp.float32)
    s = jnp.einsum('bdnqhe,bdnkhe->bdnhqk', qs.astype(jnp.float32), kn) * (e ** -0.5)
    qi = jnp.arange(radius)
    ki = jnp.arange(3 * radius)
    band = jnp.abs(ki[None, :] - radius - qi[:, None]) <= radius
    key_pos = (jnp.arange(nb)[:, None] - 1) * radius + ki[None, :]
    valid = (key_pos >= 0) & (key_pos < L)
    mask = band[None, :, :] & valid[:, None, :]
    s = jnp.where(mask[:, None, :, :], s, NEG_INF)
    lse = jax.nn.logsumexp(s, axis=-1)
    p = jnp.exp(s - lse[..., None])
    o = jnp.einsum('bdnhqk,bdnkhe->bdnqhe', p, vn)
    o = o.reshape(bsz, dilation, lp, n_h, e)[:, :, :L]
    o = o.transpose(0, 2, 1, 3, 4).reshape(bsz, s_len, n_h, e)
    lse = lse.transpose(0, 1, 2, 4, 3).reshape(bsz, dilation, lp, n_h)[:, :, :L]
    lse = lse.transpose(0, 2, 1, 3).reshape(bsz, s_len, n_h)
    return o, lse


def dense_attention_blocks(q, k, v):
    bsz, s_len, n_h, e = q.shape
    ev = v.shape[-1]
    nq = s_len // Q_BLOCK
    qb = q.reshape(bsz, nq, Q_BLOCK, n_h, e).transpose(1, 0, 2, 3, 4)
    kf = k.astype(jnp.float32)
    vf = v.astype(jnp.float32)
    scale = e ** -0.5

    def one_block(q_blk):
        s = jnp.einsum('bqhe,bkhe->bhqk', q_blk.astype(jnp.float32), kf) * scale
        p = jax.nn.softmax(s, axis=-1)
        return jnp.einsum('bhqk,bkhv->bqhv', p, vf)

    o = lax.map(one_block, qb)
    return o.transpose(1, 0, 2, 3, 4).reshape(bsz, s_len, n_h, ev).astype(q.dtype)


def even_mixer(h, cos_a, sin_a, w_in, w_out, q_norm, k_norm, b_mix):
    bsz, s_len, _ = h.shape
    z = h @ w_in
    q, k, v, f = jnp.split(z, [A_WIDTH, 2 * A_WIDTH, 3 * A_WIDTH], axis=-1)
    q = rms_norm(q.reshape(bsz, s_len, N_HEADS_A, HEAD_DIM), q_norm)
    k = rms_norm(k.reshape(bsz, s_len, N_HEADS_A, HEAD_DIM), k_norm)
    v = v.reshape(bsz, s_len, N_HEADS_A, HEAD_DIM)
    q = jnp.concatenate([apply_rotary(q[..., :ROT_DIM_A], cos_a, sin_a), q[..., ROT_DIM_A:]], axis=-1)
    k = jnp.concatenate([apply_rotary(k[..., :ROT_DIM_A], cos_a, sin_a), k[..., ROT_DIM_A:]], axis=-1)
    outs, lses = [], []
    for window, dilation in DILATED_BRANCHES:
        o, lse = dilated_window_branch(q, k, v, dilation, window // (2 * dilation))
        outs.append(o)
        lses.append(lse)
    wts = jax.nn.softmax(jnp.stack(lses, axis=0), axis=0)
    a_out = jnp.sum(wts[..., None] * jnp.stack(outs, axis=0), axis=0).reshape(bsz, s_len, A_WIDTH)
    fg = f.astype(jnp.float32).reshape(bsz, s_len, FNET_GROUPS, FNET_GROUP_DIM).transpose(0, 2, 1, 3)
    spec = jnp.fft.fft2(fg, norm='ortho').real
    b_out = jnp.einsum('bgsc,gcd->bsgd', spec, b_mix.astype(jnp.float32)).reshape(bsz, s_len, B_WIDTH)
    merged = jnp.concatenate([a_out, b_out], axis=-1).astype(h.dtype)
    return merged @ w_out


def _linear_recurrence(left, right):
    a_l, b_l = left
    a_r, b_r = right
    return a_l * a_r, a_r * b_l + b_r


def s5_bidirectional(u, lam_re, lam_im, log_step, b_re, b_im, c_re, c_im, d_skip, w_glu, b_glu):
    bsz, s_len, _ = u.shape
    uf = u.astype(jnp.float32)
    ug = uf.reshape(bsz, s_len, S5_GROUPS, S5_GROUP_DIM)
    lam = lax.complex(lam_re.astype(jnp.float32), lam_im.astype(jnp.float32))
    step = jnp.exp(log_step.astype(jnp.float32))[..., None]
    lam_bar = jnp.exp(lam * step)
    b_bar = ((lam_bar - 1.0) / lam)[..., None] * lax.complex(b_re.astype(jnp.float32), b_im.astype(jnp.float32))
    c_mat = lax.complex(c_re.astype(jnp.float32), c_im.astype(jnp.float32))
    y = uf * d_skip.astype(jnp.float32)
    for direction, reverse in ((0, False), (1, True)):
        bu = jnp.einsum('bsgh,gph->bsgp', ug, b_bar[direction])
        a = jnp.broadcast_to(lam_bar[direction][None, None], (1, s_len, S5_GROUPS, S5_STATE))
        _, states = lax.associative_scan(_linear_recurrence, (a, bu), reverse=reverse, axis=1)
        y = y + jnp.einsum('bsgp,ghp->bsgh', states, c_mat[direction]).real.reshape(bsz, s_len, D_WIDTH)
    z = jax.nn.gelu(y)
    out = z * jax.nn.sigmoid(z @ w_glu.astype(jnp.float32) + b_glu.astype(jnp.float32))
    return out.astype(u.dtype)


def odd_mixer(h, cos_c, sin_c, w_in, w_out, q_lat_norm, w_q_up, kv_lat_norm, w_kv_up, q_norm, k_norm,
              lam_re, lam_im, log_step, b_re, b_im, c_re, c_im, d_skip, w_glu, b_glu):
    bsz, s_len, _ = h.shape
    z = h @ w_in
    q_lat, kv_lat, k_pe, u = jnp.split(z, [C_Q_RANK, C_Q_RANK + C_KV_RANK, C_Q_RANK + C_KV_RANK + C_ROPE_DIM], axis=-1)
    q = (rms_norm(q_lat, q_lat_norm) @ w_q_up).reshape(bsz, s_len, N_HEADS_C, C_QK_DIM)
    kv = (rms_norm(kv_lat, kv_lat_norm) @ w_kv_up).reshape(bsz, s_len, N_HEADS_C, C_NOPE_DIM + C_V_DIM)
    k_nope, v = kv[..., :C_NOPE_DIM], kv[..., C_NOPE_DIM:]
    k_pe = jnp.broadcast_to(k_pe[:, :, None, :], (bsz, s_len, N_HEADS_C, C_ROPE_DIM))
    k = jnp.concatenate([k_nope, k_pe], axis=-1)
    q = rms_norm(q, q_norm)
    k = rms_norm(k, k_norm)
    q = jnp.concatenate([q[..., :C_NOPE_DIM], apply_rotary(q[..., C_NOPE_DIM:], cos_c, sin_c)], axis=-1)
    k = jnp.concatenate([k[..., :C_NOPE_DIM], apply_rotary(k[..., C_NOPE_DIM:], cos_c, sin_c)], axis=-1)
    c_out = dense_attention_blocks(q, k, v).reshape(bsz, s_len, C_WIDTH)
    d_out = s5_bidirectional(u, lam_re, lam_im, log_step, b_re, b_im, c_re, c_im, d_skip, w_glu, b_glu)
    merged = jnp.concatenate([c_out, d_out.astype(c_out.dtype)], axis=-1).astype(h.dtype)
    return merged @ w_out


def setup_inputs(seed: int = 0) -> dict:
    key = jax.random.key(seed)
    ks = iter(jax.random.split(key, 40))
    f32 = jnp.float32

    def normal(shape, scale):
        return jax.random.normal(next(ks), shape, f32) * scale

    def gain(shape):
        return 1.0 + normal(shape, 0.02)

    ne, no = (DEPTH + 1) // 2, DEPTH // 2
    x = normal((BATCH, SEQ, D_MODEL), 1.0)
    start = jax.random.randint(next(ks), (BATCH, 1), 0, MAX_POS_OFFSET, dtype=jnp.int32)
    positions = start + jnp.arange(SEQ, dtype=jnp.int32)[None, :]
    d_in, f_in = D_MODEL ** -0.5, D_FF ** -0.5
    return {
        'x': x,
        'positions': positions,
        'ffn1_norm': gain((DEPTH, D_MODEL)),
        'ffn1_w_gate': normal((DEPTH, D_MODEL, D_FF), d_in),
        'ffn1_w_up': normal((DEPTH, D_MODEL, D_FF), d_in),
        'ffn1_w_down': normal((DEPTH, D_FF, D_MODEL), f_in),
        'mix_norm': gain((DEPTH, D_MODEL)),
        'ffn2_norm': gain((DEPTH, D_MODEL)),
        'ffn2_w_gate': normal((DEPTH, D_MODEL, D_FF), d_in),
        'ffn2_w_up': normal((DEPTH, D_MODEL, D_FF), d_in),
        'ffn2_w_down': normal((DEPTH, D_FF, D_MODEL), f_in),
        'ab_w_in': normal((ne, D_MODEL, EVEN_IN), d_in),
        'ab_w_out': normal((ne, EVEN_OUT, D_MODEL), EVEN_OUT ** -0.5),
        'a_q_norm': gain((ne, HEAD_DIM)),
        'a_k_norm': gain((ne, HEAD_DIM)),
        'b_w_mix': normal((ne, FNET_GROUPS, FNET_GROUP_DIM, FNET_GROUP_DIM), FNET_GROUP_DIM ** -0.5),
        'cd_w_in': normal((no, D_MODEL, ODD_IN), d_in),
        'cd_w_out': normal((no, ODD_OUT, D_MODEL), ODD_OUT ** -0.5),
        'c_q_lat_norm': gain((no, C_Q_RANK)),
        'c_w_q_up': normal((no, C_Q_RANK, N_HEADS_C * C_QK_DIM), C_Q_RANK ** -0.5),
        'c_kv_lat_norm': gain((no, C_KV_RANK)),
        'c_w_kv_up': normal((no, C_KV_RANK, N_HEADS_C * (C_NOPE_DIM + C_V_DIM)), C_KV_RANK ** -0.5),
        'c_q_norm': gain((no, C_QK_DIM)),
        'c_k_norm': gain((no, C_QK_DIM)),
        'd_lam_re': -0.5 + normal((no, 2, S5_GROUPS, S5_STATE), 0.01),
        'd_lam_im': jnp.pi * jnp.arange(S5_STATE, dtype=f32) + normal((no, 2, S5_GROUPS, S5_STATE), 0.01),
        'd_log_step': jax.random.uniform(next(ks), (no, 2, S5_GROUPS), f32, math.log(S5_MIN_STEP), math.log(S5_MAX_STEP)),
        'd_b_re': normal((no, 2, S5_GROUPS, S5_STATE, S5_GROUP_DIM), (2 * S5_GROUP_DIM) ** -0.5),
        'd_b_im': normal((no, 2, S5_GROUPS, S5_STATE, S5_GROUP_DIM), (2 * S5_GROUP_DIM) ** -0.5),
        'd_c_re': normal((no, 2, S5_GROUPS, S5_GROUP_DIM, S5_STATE), (2 * S5_STATE) ** -0.5),
        'd_c_im': normal((no, 2, S5_GROUPS, S5_GROUP_DIM, S5_STATE), (2 * S5_STATE) ** -0.5),
        'd_skip': normal((no, D_WIDTH), 1.0),
        'd_w_glu': normal((no, D_WIDTH, D_WIDTH), D_WIDTH ** -0.5),
        'd_b_glu': normal((no, D_WIDTH), 0.01),
    }


def reference(x, positions, ffn1_norm, ffn1_w_gate, ffn1_w_up, ffn1_w_down, mix_norm,
              ffn2_norm, ffn2_w_gate, ffn2_w_up, ffn2_w_down,
              ab_w_in, ab_w_out, a_q_norm, a_k_norm, b_w_mix,
              cd_w_in, cd_w_out, c_q_lat_norm, c_w_q_up, c_kv_lat_norm, c_w_kv_up, c_q_norm, c_k_norm,
              d_lam_re, d_lam_im, d_log_step, d_b_re, d_b_im, d_c_re, d_c_im, d_skip, d_w_glu, d_b_glu):
    cos_a, sin_a = rotary_tables(positions, ROT_DIM_A)
    cos_c, sin_c = rotary_tables(positions, C_ROPE_DIM)
    for layer in range(DEPTH):
        x = x + 0.5 * swiglu(rms_norm(x, ffn1_norm[layer]), ffn1_w_gate[layer], ffn1_w_up[layer], ffn1_w_down[layer])
        h = rms_norm(x, mix_norm[layer])
        i = layer // 2
        if layer % 2 == 0:
            x = x + even_mixer(h, cos_a, sin_a, ab_w_in[i], ab_w_out[i], a_q_norm[i], a_k_norm[i], b_w_mix[i])
        else:
            x = x + odd_mixer(h, cos_c, sin_c, cd_w_in[i], cd_w_out[i], c_q_lat_norm[i], c_w_q_up[i],
                              c_kv_lat_norm[i], c_w_kv_up[i], c_q_norm[i], c_k_norm[i],
                              d_lam_re[i], d_lam_im[i], d_log_step[i], d_b_re[i], d_b_im[i],
                              d_c_re[i], d_c_im[i], d_skip[i], d_w_glu[i], d_b_glu[i])
        x = x + 0.5 * swiglu(rms_norm(x, ffn2_norm[layer]), ffn2_w_gate[layer], ffn2_w_up[layer], ffn2_w_down[layer])
    return x
```

```python
import functools
import math

import jax
import jax.numpy as jnp
import numpy as np
from jax import lax
from jax.experimental import pallas as pl
from jax.experimental.pallas import tpu as pltpu

F32 = jnp.float32
BF16 = jnp.bfloat16

HEAD_DIM = 64
N_HEADS_A = 8
ROT_DIM_A = 16
DILATIONS = (1, 4, 16)
RADIUS = 64
FNET_GROUPS = 4
FNET_GROUP_DIM = 64
N_HEADS_C = 8
C_NOPE_DIM = 64
C_ROPE_DIM = 32
C_V_DIM = 64
C_QK_DIM = C_NOPE_DIM + C_ROPE_DIM
C_Q_RANK = 256
C_KV_RANK = 128
S5_GROUPS = 16
S5_GROUP_DIM = 16
S5_STATE = 64
ROPE_THETA = 500000.0
NORM_EPS = 1e-6
NEG_INF = -1e30
A_WIDTH = N_HEADS_A * HEAD_DIM
B_WIDTH = FNET_GROUPS * FNET_GROUP_DIM
D_WIDTH = S5_GROUPS * S5_GROUP_DIM
S5_LANES = S5_GROUPS * S5_STATE

LANES = 128
SUBLANES = 8
VMEM_LIMIT_BYTES = 56 * 1024 * 1024


def _cparams(sem):
    return pltpu.CompilerParams(dimension_semantics=sem, vmem_limit_bytes=VMEM_LIMIT_BYTES)


def _resident(shape):
    nd = len(shape)
    return pl.BlockSpec(shape, lambda *_: (0,) * nd, pipeline_mode=pl.Buffered(1))


def _rms_rows(x, g):
    ms = jnp.mean(x * x, axis=-1, keepdims=True)
    return x * lax.rsqrt(ms + NORM_EPS) * g


def _dot(a, b):
    return jnp.dot(a, b, preferred_element_type=F32)


FFN_CHUNK = 256


def _ffn_body(x_ref, g_ref, wg_ref, wu_ref, wd_ref, o_ref, *, n_chunks):
    x = x_ref[...]
    h = _rms_rows(x, g_ref[...]).astype(BF16)
    acc = jnp.zeros(x.shape, F32)
    for c in range(n_chunks):
        sl = pl.ds(c * FFN_CHUNK, FFN_CHUNK)
        gate = _dot(h, wg_ref[:, sl])
        up = _dot(h, wu_ref[:, sl])
        act = (gate * jax.nn.sigmoid(gate) * up).astype(BF16)
        acc = acc + _dot(act, wd_ref[sl, :])
    o_ref[...] = x + 0.5 * acc


def _ffn(x2, g, wg, wu, wd, *, tm):
    t, d = x2.shape
    f = wg.shape[1]
    assert t % tm == 0 and f % FFN_CHUNK == 0
    return pl.pallas_call(
        functools.partial(_ffn_body, n_chunks=f // FFN_CHUNK),
        grid=(t // tm,),
        in_specs=[
            pl.BlockSpec((tm, d), lambda i: (i, 0)),
            _resident((1, d)),
            _resident((d, f)),
            _resident((d, f)),
            _resident((f, d)),
        ],
        out_specs=pl.BlockSpec((tm, d), lambda i: (i, 0)),
        out_shape=jax.ShapeDtypeStruct((t, d), F32),
        compiler_params=_cparams(("arbitrary",)),
        name="ffn",
    )(x2, g, wg, wu, wd)


def _rotary_block(t, cos, s_up, s_dn, shift):
    return (t * cos + pltpu.roll(t, LANES - shift, axis=1) * s_up
            + pltpu.roll(t, shift, axis=1) * s_dn)


def _even_in_body(x_ref, g_ref, w_ref, pm_ref, gq_ref, gk_ref, cos_ref, sup_ref, sdn_ref,
                  dcos_ref, dsin_ref, mbd_ref,
                  q_ref, k_ref, v_ref, zr_ref, zi_ref, wc_scr):
    first = jnp.logical_and(pl.program_id(0) == 0, pl.program_id(1) == 0)

    @pl.when(first)
    def _():
        mbd = mbd_ref[...]
        wc_scr[:, :B_WIDTH] = _dot(dcos_ref[...], mbd).astype(BF16)
        wc_scr[:, B_WIDTH:] = _dot(dsin_ref[...], mbd).astype(BF16)

    h = _rms_rows(x_ref[...], g_ref[...]).astype(BF16)
    z = _dot(h, w_ref[...])
    cos, s_up, s_dn = cos_ref[...], sup_ref[...], sdn_ref[...]
    pm = pm_ref[...]

    def head_norm_rot(t, gain, scale):
        ms = _dot((t * t).astype(BF16), pm)
        tn = t * lax.rsqrt(ms + NORM_EPS) * gain
        blocks = [_rotary_block(tn[:, c * LANES:(c + 1) * LANES], cos, s_up, s_dn, ROT_DIM_A // 2)
                  for c in range(A_WIDTH // LANES)]
        out = jnp.concatenate(blocks, axis=1)
        return out * scale if scale != 1.0 else out

    q_ref[...] = head_norm_rot(z[:, :A_WIDTH], gq_ref[...], HEAD_DIM ** -0.5)
    k_ref[...] = head_norm_rot(z[:, A_WIDTH:2 * A_WIDTH], gk_ref[...], 1.0)
    v_ref[...] = z[:, 2 * A_WIDTH:3 * A_WIDTH]
    zz = _dot(z[:, 3 * A_WIDTH:].astype(BF16), wc_scr[...])
    zr_ref[...] = zz[:, :B_WIDTH]
    zi_ref[...] = zz[:, B_WIDTH:]


def _even_in(x, g, w_in, pm, gq, gk, cos, s_up, s_dn, dcos, dsin, mbd, *, ts):
    b, s, d = x.shape
    tok = lambda w: pl.BlockSpec((None, ts, w), lambda bi, i: (bi, i, 0))
    return pl.pallas_call(
        _even_in_body,
        grid=(b, s // ts),
        in_specs=[tok(d), _resident((1, d)), _resident(w_in.shape), _resident(pm.shape),
                  _resident((1, A_WIDTH)), _resident((1, A_WIDTH)),
                  tok(LANES), tok(LANES), tok(LANES),
                  _resident(dcos.shape), _resident(dsin.shape), _resident(mbd.shape)],
        out_specs=[tok(A_WIDTH), tok(A_WIDTH), tok(A_WIDTH), tok(B_WIDTH), tok(B_WIDTH)],
        out_shape=[jax.ShapeDtypeStruct((b, s, A_WIDTH), F32)] * 3
        + [jax.ShapeDtypeStruct((b, s, B_WIDTH), F32)] * 2,
        scratch_shapes=[pltpu.VMEM((B_WIDTH, 2 * B_WIDTH), BF16)],
        compiler_params=_cparams(("arbitrary", "arbitrary")),
        name="even_in",
    )(x, g, w_in, pm, gq, gk, cos, s_up, s_dn, dcos, dsin, mbd)


DIL_TQ = 128
DIL_TK = DIL_TQ + 2 * RADIUS


def _dilated_body(q_ref, k_ref, v_ref, o_ref, acc_a, acc_b, m_a, m_b, *, seq):
    lane = lax.broadcasted_iota(jnp.int32, (1, LANES), 1)
    is_a = lane < HEAD_DIM
    acc_a[...] = jnp.zeros(acc_a.shape, F32)
    acc_b[...] = jnp.zeros(acc_b.shape, F32)
    m_a[...] = jnp.full(m_a.shape, NEG_INF, F32)
    m_b[...] = jnp.full(m_b.shape, NEG_INF, F32)
    qq = lax.broadcasted_iota(jnp.int32, (DIL_TQ, DIL_TK), 0)
    kk = lax.broadcasted_iota(jnp.int32, (DIL_TQ, DIL_TK), 1)

    for dil in DILATIONS:
        sub_len = seq // dil
        n_tiles = sub_len // DIL_TQ

        def rows(start, n, dil=dil):
            return pl.ds(start, n, stride=dil) if dil > 1 else pl.ds(start, n)

        def tile(idx, carry, dil=dil, n_tiles=n_tiles, sub_len=sub_len, rows=rows):
            res = idx // n_tiles
            i = idx - res * n_tiles
            q0 = i * DIL_TQ
            k0 = jnp.clip(q0 - RADIUS, 0, sub_len - DIL_TK)
            q_rows = rows(res + dil * q0, DIL_TQ)
            k_rows = rows(res + dil * k0, DIL_TK)
            q2 = q_ref[q_rows, :]
            k2 = k_ref[k_rows, :].astype(BF16)
            v2 = v_ref[k_rows, :]
            band = jnp.abs((k0 + kk) - (q0 + qq)) <= RADIUS
            for head_is_a, acc, m in ((True, acc_a, m_a), (False, acc_b, m_b)):
                sel = is_a if head_is_a else jnp.logical_not(is_a)
                qh = jnp.where(sel, q2, 0.0).astype(BF16)
                vh = jnp.where(sel, v2, 1.0).astype(BF16)
                sc = lax.dot_general(qh, k2, (((1,), (1,)), ((), ())),
                                     preferred_element_type=F32)
                sc = jnp.where(band, sc, NEG_INF)
                m_old = m[q_rows, :]
                m_new = jnp.maximum(m_old, jnp.max(sc, axis=1, keepdims=True))
                p = jnp.exp(sc - m_new[:, :1])
                alpha = jnp.exp(m_old - m_new)
                acc[q_rows, :] = acc[q_rows, :] * alpha + _dot(p.astype(BF16), vh)
                m[q_rows, :] = m_new
            return carry

        lax.fori_loop(0, dil * n_tiles, tile, 0)

    a = acc_a[...]
    bb = acc_b[...]
    out_a = a / pltpu.roll(a, HEAD_DIM, axis=1)
    out_b = bb / pltpu.roll(bb, HEAD_DIM, axis=1)
    o_ref[...] = jnp.where(is_a, out_a, out_b).astype(o_ref.dtype)


def _dilated(q, k, v):
    b, s, w = q.shape
    assert s % (DILATIONS[-1] * DIL_TQ) == 0 and s // DILATIONS[-1] >= DIL_TK
    spec = pl.BlockSpec((None, s, LANES), lambda bi, p: (bi, 0, p))
    return pl.pallas_call(
        functools.partial(_dilated_body, seq=s),
        grid=(b, w // LANES),
        in_specs=[spec, spec, spec],
        out_specs=spec,
        out_shape=jax.ShapeDtypeStruct((b, s, w), BF16),
        scratch_shapes=[pltpu.VMEM((s, LANES), F32)] * 4,
        compiler_params=_cparams(("arbitrary", "arbitrary")),
        name="dilated",
    )(q, k, v)


FFT_N2 = 128


def _fft1_body(zr_ref, zi_ref, f_ref, twc_ref, tws_ref, yr_ref, yi_ref, *, n1):
    zz = jnp.concatenate([zr_ref[...], zi_ref[...]], axis=0).astype(BF16)
    y = _dot(f_ref[...], zz)
    yr, yi = y[:n1], y[n1:]
    c, s = twc_ref[...], tws_ref[...]
    yr_ref[...] = yr * c + yi * s
    yi_ref[...] = yi * c - yr * s


def _fft1(zr, zi, fmat, twc, tws, *, wb):
    b, n1, w = zr.shape
    dat = pl.BlockSpec((None, n1, wb), lambda bi, j: (bi, 0, j))
    tw = pl.BlockSpec((n1, wb), lambda bi, j: (0, j))
    return pl.pallas_call(
        functools.partial(_fft1_body, n1=n1),
        grid=(b, w // wb),
        in_specs=[dat, dat, _resident(fmat.shape), tw, tw],
        out_specs=[dat, dat],
        out_shape=[jax.ShapeDtypeStruct((b, n1, w), F32)] * 2,
        compiler_params=_cparams(("arbitrary", "arbitrary")),
        name="fft_stage1",
    )(zr, zi, fmat, twc, tws)


def _fft2_body(yr_ref, yi_ref, f_ref, o_ref, *, kb):
    f = f_ref[...]
    for j in range(kb):
        rs = pl.ds(j * FFT_N2, FFT_N2)
        yy = jnp.concatenate([yr_ref[rs, :], yi_ref[rs, :]], axis=0).astype(BF16)
        o_ref[:, j * B_WIDTH:(j + 1) * B_WIDTH] = _dot(f, yy).astype(o_ref.dtype)


def _fft2(yr, yi, fmat, *, kb):
    b, rows, w = yr.shape
    n1 = rows // FFT_N2
    dat = pl.BlockSpec((None, kb * FFT_N2, w), lambda bi, j: (bi, j, 0))
    return pl.pallas_call(
        functools.partial(_fft2_body, kb=kb),
        grid=(b, n1 // kb),
        in_specs=[dat, dat, _resident(fmat.shape)],
        out_specs=pl.BlockSpec((None, FFT_N2, kb * w), lambda bi, j: (bi, 0, j)),
        out_shape=jax.ShapeDtypeStruct((b, FFT_N2, n1 * w), BF16),
        compiler_params=_cparams(("arbitrary", "arbitrary")),
        name="fft_stage2",
    )(yr, yi, fmat)


def _out_proj_body(x_ref, a_ref, b_ref, wa_ref, wb_ref, o_ref):
    o_ref[...] = x_ref[...] + _dot(a_ref[...], wa_ref[...]) + _dot(b_ref[...], wb_ref[...])


def _out_proj(x, a, bmat, wa, wb, *, ts, b_spec):
    b, s, d = x.shape
    tok = lambda w: pl.BlockSpec((None, ts, w), lambda bi, i: (bi, i, 0))
    return pl.pallas_call(
        _out_proj_body,
        grid=(b, s // ts),
        in_specs=[tok(d), tok(a.shape[-1]), b_spec, _resident(wa.shape), _resident(wb.shape)],
        out_specs=tok(d),
        out_shape=jax.ShapeDtypeStruct((b, s, d), F32),
        compiler_params=_cparams(("arbitrary", "arbitrary")),
        name="out_proj",
    )(x, a, bmat, wa, wb)


def _odd_in_body(x_ref, g_ref, w_ref, gql_ref, wq_ref, gkvl_ref, wkv_ref, gq_ref, gk_ref,
                 cos_ref, sup_ref, sdn_ref, q_ref, k_ref, v_ref, u_ref):
    h = _rms_rows(x_ref[...], g_ref[...]).astype(BF16)
    z = _dot(h, w_ref[...])
    cos, s_up, s_dn = cos_ref[...], sup_ref[...], sdn_ref[...]
    kv_off = C_Q_RANK
    pe_off = kv_off + C_KV_RANK
    u_off = pe_off + LANES
    q_lat = _rms_rows(z[:, :C_Q_RANK], gql_ref[...]).astype(BF16)
    kv_lat = _rms_rows(z[:, kv_off:pe_off], gkvl_ref[...]).astype(BF16)
    k_pe = z[:, pe_off:u_off]
    qf = _dot(q_lat, wq_ref[...])
    kvf = _dot(kv_lat, wkv_ref[...])
    gq, gk = gq_ref[...], gk_ref[...]

    def head(t, gain, scale):
        ms = jnp.sum(t * t, axis=-1, keepdims=True) * (1.0 / C_QK_DIM)
        tn = t * lax.rsqrt(ms + NORM_EPS) * gain
        out = _rotary_block(tn, cos, s_up, s_dn, C_ROPE_DIM // 2)
        return out * scale if scale != 1.0 else out

    for hd in range(N_HEADS_C):
        sl = slice(hd * LANES, (hd + 1) * LANES)
        q_ref[:, sl] = head(qf[:, sl], gq, C_QK_DIM ** -0.5).astype(BF16)
        k_ref[:, sl] = head(kvf[:, sl] + k_pe, gk, 1.0).astype(BF16)
    v_ref[...] = kvf[:, N_HEADS_C * LANES:].astype(BF16)
    u_ref[...] = z[:, u_off:]


def _odd_in(x, g, w_in, gql, wq, gkvl, wkv, gq, gk, cos, s_up, s_dn, *, ts):
    b, s, d = x.shape
    tok = lambda w: pl.BlockSpec((None, ts, w), lambda bi, i: (bi, i, 0))
    hw = N_HEADS_C * LANES
    return pl.pallas_call(
        _odd_in_body,
        grid=(b, s // ts),
        in_specs=[tok(d), _resident((1, d)), _resident(w_in.shape), _resident(gql.shape),
                  _resident(wq.shape), _resident(gkvl.shape), _resident(wkv.shape),
                  _resident(gq.shape), _resident(gk.shape), tok(LANES), tok(LANES), tok(LANES)],
        out_specs=[tok(hw), tok(hw), tok(N_HEADS_C * C_V_DIM),
                   pl.BlockSpec((ts, D_WIDTH), lambda bi, i: (i, bi))],
        out_shape=[jax.ShapeDtypeStruct((b, s, hw), BF16), jax.ShapeDtypeStruct((b, s, hw), BF16),
                   jax.ShapeDtypeStruct((b, s, N_HEADS_C * C_V_DIM), BF16),
                   jax.ShapeDtypeStruct((s, b * D_WIDTH), F32)],
        compiler_params=_cparams(("arbitrary", "arbitrary")),
        name="odd_in",
    )(x, g, w_in, gql, wq, gkvl, wkv, gq, gk, cos, s_up, s_dn)


def _flash_body(q_ref, k_ref, v_ref, o_ref, *, tk):
    tq = q_ref.shape[0]
    n_kv = k_ref.shape[0] // tk
    lane = lax.broadcasted_iota(jnp.int32, (1, LANES), 1)
    is_a = lane < C_V_DIM
    q_a = q_ref[:, :LANES]
    q_b = q_ref[:, LANES:]

    def step(j, carry):
        m_a, acc_a, m_b, acc_b = carry
        rs = pl.ds(pl.multiple_of(j * tk, tk), tk)
        v2 = v_ref[rs, :]
        out = []
        for qh, koff, sel, m_old, acc in ((q_a, 0, is_a, m_a, acc_a),
                                          (q_b, LANES, jnp.logical_not(is_a), m_b, acc_b)):
            kh = k_ref[rs, koff:koff + LANES]
            sc = lax.dot_general(qh, kh, (((1,), (1,)), ((), ())), preferred_element_type=F32)
            m_new = jnp.maximum(m_old, jnp.max(sc, axis=1, keepdims=True))
            p = jnp.exp(sc - m_new).astype(BF16)
            vh = jnp.where(sel, v2, jnp.ones_like(v2))
            acc = acc * jnp.exp(m_old - m_new) + _dot(p, vh)
            out += [m_new, acc]
        return tuple(out)

    m0 = jnp.full((tq, 1), NEG_INF, F32)
    a0 = jnp.zeros((tq, LANES), F32)
    _, acc_a, _, acc_b = lax.fori_loop(0, n_kv, step, (m0, a0, m0, a0))
    out_a = acc_a / pltpu.roll(acc_a, C_V_DIM, axis=1)
    out_b = acc_b / pltpu.roll(acc_b, C_V_DIM, axis=1)
    o_ref[...] = jnp.where(is_a, out_a, out_b).astype(o_ref.dtype)


def _flash(q, k, v, *, tq, tk):
    b, s, hw = q.shape
    pairs = hw // (2 * LANES)
    return pl.pallas_call(
        functools.partial(_flash_body, tk=tk),
        grid=(b, pairs, s // tq),
        in_specs=[pl.BlockSpec((None, tq, 2 * LANES), lambda bi, p, i: (bi, i, p)),
                  pl.BlockSpec((None, s, 2 * LANES), lambda bi, p, i: (bi, 0, p)),
                  pl.BlockSpec((None, s, LANES), lambda bi, p, i: (bi, 0, p))],
        out_specs=pl.BlockSpec((None, tq, LANES), lambda bi, p, i: (bi, i, p)),
        out_shape=jax.ShapeDtypeStruct((b, s, pairs * LANES), BF16),
        compiler_params=_cparams(("arbitrary", "arbitrary", "arbitrary")),
        name="flash",
    )(q, k, v)


def _s5_body(u_ref, bm_ref, cm_ref, lam_ref, y_ref, bu_scr, xs_scr, st_scr, *, tc, nb):
    direction = pl.program_id(0)

    @pl.when(pl.program_id(1) == 0)
    def _():
        st_scr[...] = jnp.zeros(st_scr.shape, F32)

    bu_scr[...] = _dot(u_ref[...].astype(BF16), bm_ref[...])
    a_re = jnp.broadcast_to(lam_ref[0:1, :], (nb, S5_LANES))
    a_im = jnp.broadcast_to(lam_ref[1:2, :], (nb, S5_LANES))

    def step(t, carry):
        x_re, x_im = carry
        tt = jnp.where(direction == 0, t, tc - 1 - t)
        rs = pl.ds(pl.multiple_of(tt * nb, nb), nb)
        n_re = a_re * x_re - a_im * x_im + bu_scr[rs, :S5_LANES]
        n_im = a_re * x_im + a_im * x_re + bu_scr[rs, S5_LANES:]
        xs_scr[rs, :S5_LANES] = n_re
        xs_scr[rs, S5_LANES:] = n_im
        return n_re, n_im

    x_re, x_im = lax.fori_loop(0, tc, step, (st_scr[:, :S5_LANES], st_scr[:, S5_LANES:]))
    st_scr[:, :S5_LANES] = x_re
    st_scr[:, S5_LANES:] = x_im
    y_ref[...] = _dot(xs_scr[...].astype(BF16), cm_ref[...])


def _s5_scan(u2, bmat, cmat, lam, *, tc, nb):
    rows, dw = u2.shape
    n_chunks = rows // (tc * nb)

    def chunk(d, c):
        return jnp.where(d == 0, c, n_chunks - 1 - c)

    return pl.pallas_call(
        functools.partial(_s5_body, tc=tc, nb=nb),
        grid=(2, n_chunks),
        in_specs=[pl.BlockSpec((tc * nb, dw), lambda d, c: (chunk(d, c), 0)),
                  pl.BlockSpec((None, dw, 2 * S5_LANES), lambda d, c: (d, 0, 0)),
                  pl.BlockSpec((None, 2 * S5_LANES, dw), lambda d, c: (d, 0, 0)),
                  pl.BlockSpec((None, 2, S5_LANES), lambda d, c: (d, 0, 0))],
        out_specs=pl.BlockSpec((None, tc * nb, dw), lambda d, c: (d, chunk(d, c), 0)),
        out_shape=jax.ShapeDtypeStruct((2, rows, dw), F32),
        scratch_shapes=[pltpu.VMEM((tc * nb, 2 * S5_LANES), F32),
                        pltpu.VMEM((tc * nb, 2 * S5_LANES), F32),
                        pltpu.VMEM((nb, 2 * S5_LANES), F32)],
        compiler_params=_cparams(("arbitrary", "arbitrary")),
        name="s5_scan",
    )(u2, bmat, cmat, lam)


def _s5_glu_body(y_ref, u_ref, dskip_ref, w_ref, b_ref, o_ref):
    y = y_ref[0] + y_ref[1] + u_ref[...] * dskip_ref[...]
    z = jax.nn.gelu(y)
    gate = jax.nn.sigmoid(_dot(z.astype(BF16), w_ref[...]) + b_ref[...])
    o_ref[...] = (z * gate).astype(o_ref.dtype)


def _s5_glu(y, u2, dskip, w, bias, *, tr):
    rows, dw = u2.shape
    return pl.pallas_call(
        _s5_glu_body,
        grid=(rows // tr,),
        in_specs=[pl.BlockSpec((2, tr, dw), lambda i: (0, i, 0)),
                  pl.BlockSpec((tr, dw), lambda i: (i, 0)),
                  _resident((1, dw)), _resident(w.shape), _resident((1, dw))],
        out_specs=pl.BlockSpec((tr, dw), lambda i: (i, 0)),
        out_shape=jax.ShapeDtypeStruct((rows, dw), BF16),
        compiler_params=_cparams(("arbitrary",)),
        name="s5_glu",
    )(y, u2, dskip, w, bias)


def _rotary_tables(positions, rot_dim, lane_of_first, period):
    half = rot_dim // 2
    inv_freq = ROPE_THETA ** (-jnp.arange(0, rot_dim, 2, dtype=F32) / rot_dim)
    ang = positions.astype(F32)[..., None] * inv_freq
    ang = jnp.concatenate([ang, ang], axis=-1)
    cos_r, sin_r = jnp.cos(ang), jnp.sin(ang)
    lane = np.arange(LANES)
    j = (lane - lane_of_first) % period
    in_rot = (lane >= lane_of_first) & (j < rot_dim)
    idx = np.where(in_rot, j, 0)
    cos = jnp.where(in_rot, cos_r[..., idx], 1.0)
    sin = jnp.where(in_rot, sin_r[..., idx], 0.0)
    s_up = jnp.where(in_rot & (j < half), -sin, 0.0)
    s_dn = jnp.where(in_rot & (j >= half), sin, 0.0)
    return cos, s_up, s_dn


def _dft_mats(n):
    ang = 2.0 * np.pi * np.outer(np.arange(n), np.arange(n)) / n
    return np.cos(ang), np.sin(ang)


def _block_diag(blocks):
    g, r, c = blocks.shape
    eye = jnp.eye(g, dtype=blocks.dtype)
    return jnp.einsum("grc,gh->grhc", blocks, eye).reshape(g * r, g * c)


def _pick_tile(n, pref):
    t = min(n, pref)
    while n % t:
        t //= 2
    return t


def kernel(x, positions, ffn1_norm, ffn1_w_gate, ffn1_w_up, ffn1_w_down, mix_norm, ffn2_norm, ffn2_w_gate, ffn2_w_up, ffn2_w_down, ab_w_in, ab_w_out, a_q_norm, a_k_norm, b_w_mix, cd_w_in, cd_w_out, c_q_lat_norm, c_w_q_up, c_kv_lat_norm, c_w_kv_up, c_q_norm, c_k_norm, d_lam_re, d_lam_im, d_log_step, d_b_re, d_b_im, d_c_re, d_c_im, d_skip, d_w_glu, d_b_glu):
    bsz, seq, d_model = x.shape
    depth = ffn1_norm.shape[0]
    tokens = bsz * seq
    ts = _pick_tile(seq, 512)
    tm = _pick_tile(tokens, 512)
    n1 = seq // FFT_N2
    row = lambda v: v.reshape(1, -1).astype(F32)

    cos_a, sup_a, sdn_a = _rotary_tables(positions, ROT_DIM_A, 0, HEAD_DIM)
    cos_c, sup_c, sdn_c = _rotary_tables(positions, C_ROPE_DIM, C_NOPE_DIM, LANES)

    cc, sc_ = _dft_mats(FNET_GROUP_DIM)
    norm = 1.0 / math.sqrt(seq * FNET_GROUP_DIM)
    eye_g = np.eye(FNET_GROUPS)
    dcos = jnp.asarray(np.kron(eye_g, cc) * norm, BF16)
    dsin = jnp.asarray(np.kron(eye_g, -sc_) * norm, BF16)
    c1, s1 = _dft_mats(n1)
    f1 = jnp.asarray(np.block([[c1, s1], [-s1, c1]]), BF16)
    c2, s2 = _dft_mats(FFT_N2)
    f2 = jnp.asarray(np.concatenate([c2, s2], axis=1), BF16)
    tw_ang = 2.0 * np.pi * np.outer(np.arange(n1), np.arange(FFT_N2)) / seq
    twc = jnp.repeat(jnp.asarray(np.cos(tw_ang), F32), B_WIDTH, axis=1)
    tws = jnp.repeat(jnp.asarray(np.sin(tw_ang), F32), B_WIDTH, axis=1)
    head_mean = jnp.asarray(np.kron(np.eye(N_HEADS_A), np.full((HEAD_DIM, HEAD_DIM), 1.0 / HEAD_DIM)), BF16)

    x2 = x.reshape(tokens, d_model)
    for layer in range(depth):
        i = layer // 2
        x2 = _ffn(x2, row(ffn1_norm[layer]), ffn1_w_gate[layer].astype(BF16),
                  ffn1_w_up[layer].astype(BF16), ffn1_w_down[layer].astype(BF16), tm=tm)
        x3 = x2.reshape(bsz, seq, d_model)
        if layer % 2 == 0:
            mbd = _block_diag(b_w_mix[i]).astype(BF16)
            q, k, v, zr, zi = _even_in(
                x3, row(mix_norm[layer]), ab_w_in[i].astype(BF16), head_mean,
                row(jnp.tile(a_q_norm[i], N_HEADS_A)), row(jnp.tile(a_k_norm[i], N_HEADS_A)),
                cos_a, sup_a, sdn_a, dcos, dsin, mbd, ts=ts)
            a_out = _dilated(q, k, v)
            shp = (bsz, n1, FFT_N2 * B_WIDTH)
            yr, yi = _fft1(zr.reshape(shp), zi.reshape(shp), f1, twc, tws,
                           wb=_pick_tile(FFT_N2 * B_WIDTH, 4096))
            shp2 = (bsz, n1 * FFT_N2, B_WIDTH)
            spec = _fft2(yr.reshape(shp2), yi.reshape(shp2), f2, kb=_pick_tile(n1, 8))
            b_out = spec.reshape(bsz, seq, B_WIDTH)
            w_out = ab_w_out[i].astype(BF16)
            x3 = _out_proj(x3, a_out, b_out, w_out[:A_WIDTH], w_out[A_WIDTH:], ts=ts,
                           b_spec=pl.BlockSpec((None, ts, B_WIDTH), lambda bi, j: (bi, j, 0)))
        else:
            w_in = cd_w_in[i]
            pe_off = C_Q_RANK + C_KV_RANK
            k_pe_w = jnp.pad(w_in[:, pe_off:pe_off + C_ROPE_DIM],
                             ((0, 0), (C_NOPE_DIM, LANES - C_QK_DIM)))
            w_in_p = jnp.concatenate([w_in[:, :pe_off], k_pe_w, w_in[:, pe_off + C_ROPE_DIM:]],
                                     axis=1).astype(BF16)
            pad_h = lambda w, keep: jnp.pad(
                w, ((0, 0), (0, 0), (0, LANES - keep))).reshape(w.shape[0], N_HEADS_C * LANES)
            wq_p = pad_h(c_w_q_up[i].reshape(C_Q_RANK, N_HEADS_C, C_QK_DIM), C_QK_DIM).astype(BF16)
            wkv = c_w_kv_up[i].reshape(C_KV_RANK, N_HEADS_C, C_NOPE_DIM + C_V_DIM)
            wkv_p = jnp.concatenate(
                [pad_h(wkv[:, :, :C_NOPE_DIM], C_NOPE_DIM),
                 wkv[:, :, C_NOPE_DIM:].reshape(C_KV_RANK, N_HEADS_C * C_V_DIM)], axis=1).astype(BF16)
            pad_g = lambda gvec: jnp.pad(gvec, (0, LANES - C_QK_DIM)).reshape(1, LANES).astype(F32)
            q, k, v, u_t = _odd_in(
                x3, row(mix_norm[layer]), w_in_p, row(c_q_lat_norm[i]), wq_p,
                row(c_kv_lat_norm[i]), wkv_p, pad_g(c_q_norm[i]), pad_g(c_k_norm[i]),
                cos_c, sup_c, sdn_c, ts=ts)
            c_out = _flash(q, k, v, tq=_pick_tile(seq, 256), tk=_pick_tile(seq, 512))

            lam = lax.complex(d_lam_re[i].astype(F32), d_lam_im[i].astype(F32))
            step = jnp.exp(d_log_step[i].astype(F32))[..., None]
            lam_bar = jnp.exp(lam * step)
            b_bar = ((lam_bar - 1.0) / lam)[..., None] * lax.complex(
                d_b_re[i].astype(F32), d_b_im[i].astype(F32))
            to_bd = lambda m: jnp.stack([_block_diag(m[d]) for d in range(2)])
            b_t = jnp.swapaxes(b_bar, -1, -2)
            bmat = jnp.concatenate([to_bd(jnp.real(b_t)), to_bd(jnp.imag(b_t))], axis=-1).astype(BF16)
            c_re_t = jnp.swapaxes(d_c_re[i].astype(F32), -1, -2)
            c_im_t = jnp.swapaxes(d_c_im[i].astype(F32), -1, -2)
            cmat = jnp.concatenate([to_bd(c_re_t), to_bd(-c_im_t)], axis=-2).astype(BF16)
            lam2 = jnp.stack([jnp.real(lam_bar).reshape(2, S5_LANES),
                              jnp.imag(lam_bar).reshape(2, S5_LANES)], axis=1)

            u2 = u_t.reshape(seq * bsz, D_WIDTH)
            tc = _pick_tile(seq, 128)
            y = _s5_scan(u2, bmat, cmat, lam2, tc=tc, nb=bsz)
            d2 = _s5_glu(y, u2, row(d_skip[i]), d_w_glu[i].astype(BF16), row(d_b_glu[i]),
                         tr=_pick_tile(seq * bsz, 1024))
            d_out = d2.reshape(seq, bsz * D_WIDTH)
            w_out = cd_w_out[i].astype(BF16)
            x3 = _out_proj(x3, c_out, d_out, w_out[:N_HEADS_C * C_V_DIM],
                           w_out[N_HEADS_C * C_V_DIM:], ts=ts,
                           b_spec=pl.BlockSpec((ts, D_WIDTH), lambda bi, j: (j, bi)))
        x2 = x3.reshape(tokens, d_model)
        x2 = _ffn(x2, row(ffn2_norm[layer]), ffn2_w_gate[layer].astype(BF16),
                  ffn2_w_up[layer].astype(BF16), ffn2_w_down[layer].astype(BF16), tm=tm)
    return x2.reshape(bsz, seq, d_model)
```

```python
import functools
import math

import jax
import jax.numpy as jnp
import numpy as np
from jax import lax
from jax.experimental import pallas as pl
from jax.experimental.pallas import tpu as pltpu

F32 = jnp.float32
BF16 = jnp.bfloat16

HEAD_DIM = 64
N_HEADS_A = 8
ROT_DIM_A = 16
DILATIONS = (1, 4, 16)
RADIUS = 64
FNET_GROUPS = 4
FNET_GROUP_DIM = 64
N_HEADS_C = 8
C_NOPE_DIM = 64
C_ROPE_DIM = 32
C_V_DIM = 64
C_QK_DIM = C_NOPE_DIM + C_ROPE_DIM
C_Q_RANK = 256
C_KV_RANK = 128
S5_GROUPS = 16
S5_GROUP_DIM = 16
S5_STATE = 64
ROPE_THETA = 500000.0
NORM_EPS = 1e-6
NEG_INF = -1e30
A_WIDTH = N_HEADS_A * HEAD_DIM
B_WIDTH = FNET_GROUPS * FNET_GROUP_DIM
D_WIDTH = S5_GROUPS * S5_GROUP_DIM
S5_LANES = S5_GROUPS * S5_STATE

LANES = 128
SUBLANES = 8
VMEM_LIMIT_BYTES = 56 * 1024 * 1024


def _cparams(sem):
    return pltpu.CompilerParams(dimension_semantics=sem, vmem_limit_bytes=VMEM_LIMIT_BYTES)


def _resident(shape):
    nd = len(shape)
    return pl.BlockSpec(shape, lambda *_: (0,) * nd, pipeline_mode=pl.Buffered(1))


def _rms_rows(x, g):
    ms = jnp.mean(x * x, axis=-1, keepdims=True)
    return x * lax.rsqrt(ms + NORM_EPS) * g


def _dot(a, b):
    return jnp.dot(a, b, preferred_element_type=F32)


FFN_CHUNK = 256


def _ffn_body(x_ref, g_ref, wg_ref, wu_ref, wd_ref, o_ref, *, n_chunks):
    x = x_ref[...]
    h = _rms_rows(x, g_ref[...]).astype(BF16)
    acc = jnp.zeros(x.shape, F32)
    for c in range(n_chunks):
        sl = pl.ds(c * FFN_CHUNK, FFN_CHUNK)
        gate = _dot(h, wg_ref[:, sl])
        up = _dot(h, wu_ref[:, sl])
        act = (gate * jax.nn.sigmoid(gate) * up).astype(BF16)
        acc = acc + _dot(act, wd_ref[sl, :])
    o_ref[...] = x + 0.5 * acc


def _ffn(x2, g, wg, wu, wd, *, tm):
    t, d = x2.shape
    f = wg.shape[1]
    assert t % tm == 0 and f % FFN_CHUNK == 0
    return pl.pallas_call(
        functools.partial(_ffn_body, n_chunks=f // FFN_CHUNK),
        grid=(t // tm,),
        in_specs=[
            pl.BlockSpec((tm, d), lambda i: (i, 0)),
            _resident((1, d)),
            _resident((d, f)),
            _resident((d, f)),
            _resident((f, d)),
        ],
        out_specs=pl.BlockSpec((tm, d), lambda i: (i, 0)),
        out_shape=jax.ShapeDtypeStruct((t, d), F32),
        compiler_params=_cparams(("arbitrary",)),
        name="ffn",
    )(x2, g, wg, wu, wd)


def _rotary_block(t, cos, s_up, s_dn, shift):
    return (t * cos + pltpu.roll(t, LANES - shift, axis=1) * s_up
            + pltpu.roll(t, shift, axis=1) * s_dn)


def _even_in_body(x_ref, g_ref, w_ref, pm_ref, gq_ref, gk_ref, cos_ref, sup_ref, sdn_ref,
                  dcos_ref, dsin_ref, mbd_ref,
                  q_ref, k_ref, v_ref, zr_ref, zi_ref, wc_scr):
    first = jnp.logical_and(pl.program_id(0) == 0, pl.program_id(1) == 0)

    @pl.when(first)
    def _():
        mbd = mbd_ref[...]
        wc_scr[:, :B_WIDTH] = _dot(dcos_ref[...], mbd).astype(BF16)
        wc_scr[:, B_WIDTH:] = _dot(dsin_ref[...], mbd).astype(BF16)

    h = _rms_rows(x_ref[...], g_ref[...]).astype(BF16)
    z = _dot(h, w_ref[...])
    cos, s_up, s_dn = cos_ref[...], sup_ref[...], sdn_ref[...]
    pm = pm_ref[...]

    def head_norm_rot(t, gain, scale):
        ms = _dot((t * t).astype(BF16), pm)
        tn = t * lax.rsqrt(ms + NORM_EPS) * gain
        blocks = [_rotary_block(tn[:, c * LANES:(c + 1) * LANES], cos, s_up, s_dn, ROT_DIM_A // 2)
                  for c in range(A_WIDTH // LANES)]
        out = jnp.concatenate(blocks, axis=1)
        return out * scale if scale != 1.0 else out

    q_ref[...] = head_norm_rot(z[:, :A_WIDTH], gq_ref[...], HEAD_DIM ** -0.5 * math.log2(math.e))
    k_ref[...] = head_norm_rot(z[:, A_WIDTH:2 * A_WIDTH], gk_ref[...], 1.0)
    v_ref[...] = z[:, 2 * A_WIDTH:3 * A_WIDTH]
    zz = _dot(z[:, 3 * A_WIDTH:].astype(BF16), wc_scr[...])
    zr_ref[...] = zz[:, :B_WIDTH]
    zi_ref[...] = zz[:, B_WIDTH:]


def _even_in(x, g, w_in, pm, gq, gk, cos, s_up, s_dn, dcos, dsin, mbd, *, ts):
    b, s, d = x.shape
    tok = lambda w: pl.BlockSpec((None, ts, w), lambda bi, i: (bi, i, 0))
    return pl.pallas_call(
        _even_in_body,
        grid=(b, s // ts),
        in_specs=[tok(d), _resident((1, d)), _resident(w_in.shape), _resident(pm.shape),
                  _resident((1, A_WIDTH)), _resident((1, A_WIDTH)),
                  tok(LANES), tok(LANES), tok(LANES),
                  _resident(dcos.shape), _resident(dsin.shape), _resident(mbd.shape)],
        out_specs=[tok(A_WIDTH), tok(A_WIDTH), tok(A_WIDTH), tok(B_WIDTH), tok(B_WIDTH)],
        out_shape=[jax.ShapeDtypeStruct((b, s, A_WIDTH), F32)] * 3
        + [jax.ShapeDtypeStruct((b, s, B_WIDTH), F32)] * 2,
        scratch_shapes=[pltpu.VMEM((B_WIDTH, 2 * B_WIDTH), BF16)],
        compiler_params=_cparams(("arbitrary", "arbitrary")),
        name="even_in",
    )(x, g, w_in, pm, gq, gk, cos, s_up, s_dn, dcos, dsin, mbd)


DIL_TQ = 128
DIL_TK = DIL_TQ + 2 * RADIUS


DIL_UNROLL = 4
DIL_COPY_ROWS = 512


def _dilated_body(q_ref, k_ref, v_ref, o_ref, qd, kd, vd, acc_a, acc_b, m_a, m_b, bias_scr, *, seq):
    lane = lax.broadcasted_iota(jnp.int32, (1, LANES), 1)
    is_a = lane < HEAD_DIM
    qq = lax.broadcasted_iota(jnp.int32, (DIL_TQ, DIL_TK), 0)
    kk = lax.broadcasted_iota(jnp.int32, (DIL_TQ, DIL_TK), 1)
    for idx in range(3):
        bias_scr[idx] = jnp.where(jnp.abs(kk - qq - RADIUS * idx) <= RADIUS, 0.0, NEG_INF)

    order = tuple(reversed(DILATIONS))
    for dil in order:
        first, last = dil == order[0], dil == order[-1]
        sub_len = seq // dil
        n_tiles = sub_len // DIL_TQ
        copy_rows = min(DIL_COPY_ROWS, sub_len)
        chunks = sub_len // copy_rows

        def rows(res, first_row, n, dil=dil):
            if dil == 1:
                return pl.ds(pl.multiple_of(first_row, DIL_TQ), n)
            return pl.ds(res + dil * first_row, n, stride=dil)

        def deinterleave(idx, carry, dil=dil, sub_len=sub_len, rows=rows, copy_rows=copy_rows,
                         chunks=chunks):
            res = idx // chunks
            c0 = (idx - res * chunks) * copy_rows
            src = rows(res, c0, copy_rows)
            dst = pl.ds(pl.multiple_of(res * sub_len + c0, copy_rows), copy_rows)
            qd[dst, :] = q_ref[src, :].astype(BF16)
            kd[dst, :] = k_ref[src, :].astype(BF16)
            vd[dst, :] = v_ref[src, :].astype(BF16)
            return carry

        lax.fori_loop(0, dil * chunks, deinterleave, 0)

        groups = n_tiles // DIL_UNROLL

        def tiles(idx, carry, dil=dil, sub_len=sub_len, rows=rows, groups=groups,
                  first=first, last=last):
            res = idx // groups
            g = idx - res * groups
            base = res * sub_len
            loaded = []
            for u in range(DIL_UNROLL):
                q0 = (g * DIL_UNROLL + u) * DIL_TQ
                k0 = jnp.clip(q0 - RADIUS, 0, sub_len - DIL_TK)
                k_rows = pl.ds(pl.multiple_of(base + k0, RADIUS), DIL_TK)
                st_rows = rows(res, q0, DIL_TQ)
                item = dict(
                    st_rows=st_rows,
                    q2=qd[pl.ds(pl.multiple_of(base + q0, DIL_TQ), DIL_TQ), :],
                    k2=kd[k_rows, :], v2=vd[k_rows, :],
                    bias=bias_scr[lax.shift_right_logical(q0 - k0, RADIUS.bit_length() - 1)])
                if not first:
                    item.update(acc=(acc_a[st_rows, :], acc_b[st_rows, :]),
                                m=(m_a[st_rows, :], m_b[st_rows, :]))
                loaded.append(item)
            results = []
            for item in loaded:
                new_acc, new_m = [], []
                for h in range(2):
                    sel = is_a if h == 0 else jnp.logical_not(is_a)
                    qh = jnp.where(sel, item["q2"], jnp.zeros_like(item["q2"]))
                    vh = jnp.where(sel, item["v2"], jnp.ones_like(item["v2"]))
                    sc = lax.dot_general(qh, item["k2"], (((1,), (1,)), ((), ())),
                                         preferred_element_type=F32) + item["bias"]
                    row_max = jnp.max(sc, axis=1, keepdims=True)
                    if first:
                        m_new = jnp.broadcast_to(row_max, (DIL_TQ, LANES))
                        acc = _dot(jnp.exp2(sc - row_max).astype(BF16), vh)
                    else:
                        m_old = item["m"][h]
                        m_new = jnp.maximum(m_old, row_max)
                        p = jnp.exp2(sc - m_new[:, :1]).astype(BF16)
                        acc = item["acc"][h] * jnp.exp2(m_old - m_new) + _dot(p, vh)
                    new_acc.append(acc)
                    new_m.append(m_new)
                results.append((item["st_rows"], new_acc, new_m))
            for st_rows, new_acc, new_m in results:
                if last:
                    out_a = new_acc[0] / pltpu.roll(new_acc[0], HEAD_DIM, axis=1)
                    out_b = new_acc[1] / pltpu.roll(new_acc[1], HEAD_DIM, axis=1)
                    o_ref[st_rows, :] = jnp.where(is_a, out_a, out_b).astype(o_ref.dtype)
                else:
                    acc_a[st_rows, :] = new_acc[0]
                    acc_b[st_rows, :] = new_acc[1]
                    m_a[st_rows, :] = new_m[0]
                    m_b[st_rows, :] = new_m[1]
            return carry

        lax.fori_loop(0, dil * groups, tiles, 0)


def _dilated(q, k, v):
    b, s, w = q.shape
    assert DILATIONS[0] == 1
    assert s % (DILATIONS[-1] * DIL_TQ * DIL_UNROLL) == 0 and s // DILATIONS[-1] >= DIL_TK
    spec = pl.BlockSpec((None, s, LANES), lambda bi, p: (bi, 0, p))
    return pl.pallas_call(
        functools.partial(_dilated_body, seq=s),
        grid=(b, w // LANES),
        in_specs=[pl.BlockSpec((None, s, LANES), lambda bi, p: (bi, 0, p),
                               pipeline_mode=pl.Buffered(1))] * 3,
        out_specs=spec,
        out_shape=jax.ShapeDtypeStruct((b, s, w), BF16),
        scratch_shapes=[pltpu.VMEM((s, LANES), BF16)] * 3 + [pltpu.VMEM((s, LANES), F32)] * 4
        + [pltpu.VMEM((3, DIL_TQ, DIL_TK), F32)],
        compiler_params=_cparams(("arbitrary", "arbitrary")),
        name="dilated",
    )(q, k, v)


FFT_N2 = 128


def _fft1_body(zr_ref, zi_ref, f_ref, twc_ref, tws_ref, yr_ref, yi_ref, *, n1):
    zz = jnp.concatenate([zr_ref[...], zi_ref[...]], axis=0).astype(BF16)
    y = _dot(f_ref[...], zz)
    yr, yi = y[:n1], y[n1:]
    c, s = twc_ref[...], tws_ref[...]
    yr_ref[...] = yr * c + yi * s
    yi_ref[...] = yi * c - yr * s


def _fft1(zr, zi, fmat, twc, tws, *, wb):
    b, n1, w = zr.shape
    dat = pl.BlockSpec((None, n1, wb), lambda bi, j: (bi, 0, j))
    tw = pl.BlockSpec((n1, wb), lambda bi, j: (0, j))
    return pl.pallas_call(
        functools.partial(_fft1_body, n1=n1),
        grid=(b, w // wb),
        in_specs=[dat, dat, _resident(fmat.shape), tw, tw],
        out_specs=[dat, dat],
        out_shape=[jax.ShapeDtypeStruct((b, n1, w), F32)] * 2,
        compiler_params=_cparams(("arbitrary", "arbitrary")),
        name="fft_stage1",
    )(zr, zi, fmat, twc, tws)


def _fft2_body(yr_ref, yi_ref, f_ref, o_ref, *, kb):
    f = f_ref[...]
    for j in range(kb):
        rs = pl.ds(j * FFT_N2, FFT_N2)
        yy = jnp.concatenate([yr_ref[rs, :], yi_ref[rs, :]], axis=0).astype(BF16)
        o_ref[:, j * B_WIDTH:(j + 1) * B_WIDTH] = _dot(f, yy).astype(o_ref.dtype)


def _fft2(yr, yi, fmat, *, kb):
    b, rows, w = yr.shape
    n1 = rows // FFT_N2
    dat = pl.BlockSpec((None, kb * FFT_N2, w), lambda bi, j: (bi, j, 0))
    return pl.pallas_call(
        functools.partial(_fft2_body, kb=kb),
        grid=(b, n1 // kb),
        in_specs=[dat, dat, _resident(fmat.shape)],
        out_specs=pl.BlockSpec((None, FFT_N2, kb * w), lambda bi, j: (bi, 0, j)),
        out_shape=jax.ShapeDtypeStruct((b, FFT_N2, n1 * w), BF16),
        compiler_params=_cparams(("arbitrary", "arbitrary")),
        name="fft_stage2",
    )(yr, yi, fmat)


def _out_proj_body(x_ref, a_ref, b_ref, wa_ref, wb_ref, o_ref):
    o_ref[...] = x_ref[...] + _dot(a_ref[...], wa_ref[...]) + _dot(b_ref[...], wb_ref[...])


def _out_proj(x, a, bmat, wa, wb, *, ts, b_spec):
    b, s, d = x.shape
    tok = lambda w: pl.BlockSpec((None, ts, w), lambda bi, i: (bi, i, 0))
    return pl.pallas_call(
        _out_proj_body,
        grid=(b, s // ts),
        in_specs=[tok(d), tok(a.shape[-1]), b_spec, _resident(wa.shape), _resident(wb.shape)],
        out_specs=tok(d),
        out_shape=jax.ShapeDtypeStruct((b, s, d), F32),
        compiler_params=_cparams(("arbitrary", "arbitrary")),
        name="out_proj",
    )(x, a, bmat, wa, wb)


def _odd_in_body(x_ref, g_ref, w_ref, gql_ref, wq_ref, gkvl_ref, wkv_ref, gq_ref, gk_ref,
                 cos_ref, sup_ref, sdn_ref, q_ref, k_ref, v_ref, u_ref):
    h = _rms_rows(x_ref[...], g_ref[...]).astype(BF16)
    z = _dot(h, w_ref[...])
    cos, s_up, s_dn = cos_ref[...], sup_ref[...], sdn_ref[...]
    kv_off = C_Q_RANK
    pe_off = kv_off + C_KV_RANK
    u_off = pe_off + LANES
    q_lat = _rms_rows(z[:, :C_Q_RANK], gql_ref[...]).astype(BF16)
    kv_lat = _rms_rows(z[:, kv_off:pe_off], gkvl_ref[...]).astype(BF16)
    k_pe = z[:, pe_off:u_off]
    qf = _dot(q_lat, wq_ref[...])
    kvf = _dot(kv_lat, wkv_ref[...])
    gq, gk = gq_ref[...], gk_ref[...]

    def head(t, gain, scale):
        ms = jnp.sum(t * t, axis=-1, keepdims=True) * (1.0 / C_QK_DIM)
        tn = t * lax.rsqrt(ms + NORM_EPS) * gain
        out = _rotary_block(tn, cos, s_up, s_dn, C_ROPE_DIM // 2)
        return out * scale if scale != 1.0 else out

    for hd in range(N_HEADS_C):
        sl = slice(hd * LANES, (hd + 1) * LANES)
        q_ref[:, sl] = head(qf[:, sl], gq, C_QK_DIM ** -0.5 * math.log2(math.e)).astype(BF16)
        k_ref[:, sl] = head(kvf[:, sl] + k_pe, gk, 1.0).astype(BF16)
    v_ref[...] = kvf[:, N_HEADS_C * LANES:].astype(BF16)
    u_ref[...] = z[:, u_off:]


def _odd_in(x, g, w_in, gql, wq, gkvl, wkv, gq, gk, cos, s_up, s_dn, *, ts):
    b, s, d = x.shape
    tok = lambda w: pl.BlockSpec((None, ts, w), lambda bi, i: (bi, i, 0))
    hw = N_HEADS_C * LANES
    return pl.pallas_call(
        _odd_in_body,
        grid=(b, s // ts),
        in_specs=[tok(d), _resident((1, d)), _resident(w_in.shape), _resident(gql.shape),
                  _resident(wq.shape), _resident(gkvl.shape), _resident(wkv.shape),
                  _resident(gq.shape), _resident(gk.shape), tok(LANES), tok(LANES), tok(LANES)],
        out_specs=[tok(hw), tok(hw), tok(N_HEADS_C * C_V_DIM),
                   pl.BlockSpec((ts, D_WIDTH), lambda bi, i: (i, bi))],
        out_shape=[jax.ShapeDtypeStruct((b, s, hw), BF16), jax.ShapeDtypeStruct((b, s, hw), BF16),
                   jax.ShapeDtypeStruct((b, s, N_HEADS_C * C_V_DIM), BF16),
                   jax.ShapeDtypeStruct((s, b * D_WIDTH), F32)],
        compiler_params=_cparams(("arbitrary", "arbitrary")),
        name="odd_in",
    )(x, g, w_in, gql, wq, gkvl, wkv, gq, gk, cos, s_up, s_dn)


FLASH_ONES_ROWS = 16


def _flash_body(q_ref, k_ref, vt_ref, o_ref, s_scr, p_scr, *, tk):
    tq = q_ref.shape[0]
    n_kv = k_ref.shape[0] // tk
    heads = ((0, 0), (LANES, C_V_DIM))
    ones = jnp.ones((FLASH_ONES_ROWS, tk), BF16)

    def scores(t, slot):
        rs = pl.ds(pl.multiple_of(t * tk, tk), tk)
        col_max = []
        for h, (off, _) in enumerate(heads):
            s = lax.dot_general(k_ref[rs, off:off + LANES], q_ref[:, off:off + LANES],
                                (((1,), (1,)), ((), ())), preferred_element_type=F32)
            s_scr[slot, h] = s
            col_max.append(jnp.max(s, axis=0, keepdims=True))
        return tuple(col_max)

    def softmax(slot, m2, col_max2):
        m_out, alpha_out = [], []
        for h in range(2):
            m_new = jnp.maximum(m2[h], col_max2[h])
            p_scr[slot, h] = jnp.exp2(s_scr[slot, h] - m_new).astype(BF16)
            alpha_out.append(jnp.exp2(m2[h] - m_new))
            m_out.append(m_new)
        return tuple(m_out), tuple(alpha_out)

    def accumulate(t, slot, acc2, alpha2):
        cs = pl.ds(pl.multiple_of(t * tk, tk), tk)
        out = []
        for h, (_, voff) in enumerate(heads):
            vt = jnp.concatenate([vt_ref[voff:voff + C_V_DIM, cs], ones], axis=0)
            out.append(acc2[h] * alpha2[h] + _dot(vt, p_scr[slot, h]))
        return tuple(out)

    def tick(t, parity, carry):
        col_prev, alpha_prev, m2, acc2 = carry
        col_new = scores(t, parity)
        m2, alpha_new = softmax(1 - parity, m2, col_prev)
        acc2 = accumulate(t - 2, parity, acc2, alpha_prev)
        return col_new, alpha_new, m2, acc2

    m0 = jnp.full((1, tq), NEG_INF, F32)
    a0 = jnp.zeros((C_V_DIM + FLASH_ONES_ROWS, tq), F32)
    col0 = scores(0, 0)
    col1 = scores(1, 1)
    m2, alpha0 = softmax(0, (m0, m0), col0)

    def two_ticks(i, carry):
        t = 2 * i + 2
        return tick(t + 1, 1, tick(t, 0, carry))

    col_prev, alpha_prev, m2, acc2 = lax.fori_loop(
        0, (n_kv - 2) // 2, two_ticks, (col1, alpha0, m2, (a0, a0)))
    m2, alpha_new = softmax(1, m2, col_prev)
    acc2 = accumulate(n_kv - 2, 0, acc2, alpha_prev)
    acc2 = accumulate(n_kv - 1, 1, acc2, alpha_new)
    out_t = jnp.concatenate([acc[:C_V_DIM] / acc[C_V_DIM:C_V_DIM + 1] for acc in acc2], axis=0)
    o_ref[...] = out_t.T.astype(o_ref.dtype)


def _flash(q, k, vt, *, tq, tk):
    b, s, hw = q.shape
    pairs = hw // (2 * LANES)
    assert (s // tk) % 2 == 0 and s // tk >= 2
    return pl.pallas_call(
        functools.partial(_flash_body, tk=tk),
        grid=(b, pairs, s // tq),
        in_specs=[pl.BlockSpec((None, tq, 2 * LANES), lambda bi, p, i: (bi, i, p)),
                  pl.BlockSpec((None, s, 2 * LANES), lambda bi, p, i: (bi, 0, p)),
                  pl.BlockSpec((None, 2 * C_V_DIM, s), lambda bi, p, i: (bi, p, 0))],
        out_specs=pl.BlockSpec((None, tq, LANES), lambda bi, p, i: (bi, i, p)),
        out_shape=jax.ShapeDtypeStruct((b, s, pairs * LANES), BF16),
        scratch_shapes=[pltpu.VMEM((2, 2, tk, tq), F32), pltpu.VMEM((2, 2, tk, tq), BF16)],
        compiler_params=_cparams(("arbitrary", "arbitrary", "arbitrary")),
        name="flash",
    )(q, k, vt)


def _s5_body(u_ref, bm_ref, cm_ref, lam_ref, y_ref, bu_scr, xs_scr, st_scr, *, tc, nb):
    direction = pl.program_id(0)

    @pl.when(pl.program_id(1) == 0)
    def _():
        st_scr[...] = jnp.zeros(st_scr.shape, F32)

    bu_scr[...] = _dot(u_ref[...].astype(BF16), bm_ref[...])
    a_re = jnp.broadcast_to(lam_ref[0:1, :], (nb, S5_LANES))
    a_im = jnp.broadcast_to(lam_ref[1:2, :], (nb, S5_LANES))

    def step(t, carry):
        x_re, x_im = carry
        tt = jnp.where(direction == 0, t, tc - 1 - t)
        rs = pl.ds(pl.multiple_of(tt * nb, nb), nb)
        n_re = a_re * x_re - a_im * x_im + bu_scr[rs, :S5_LANES]
        n_im = a_re * x_im + a_im * x_re + bu_scr[rs, S5_LANES:]
        xs_scr[rs, :S5_LANES] = n_re
        xs_scr[rs, S5_LANES:] = n_im
        return n_re, n_im

    x_re, x_im = lax.fori_loop(0, tc, step, (st_scr[:, :S5_LANES], st_scr[:, S5_LANES:]))
    st_scr[:, :S5_LANES] = x_re
    st_scr[:, S5_LANES:] = x_im
    y_ref[...] = _dot(xs_scr[...].astype(BF16), cm_ref[...])


def _s5_scan(u2, bmat, cmat, lam, *, tc, nb):
    rows, dw = u2.shape
    n_chunks = rows // (tc * nb)

    def chunk(d, c):
        return jnp.where(d == 0, c, n_chunks - 1 - c)

    return pl.pallas_call(
        functools.partial(_s5_body, tc=tc, nb=nb),
        grid=(2, n_chunks),
        in_specs=[pl.BlockSpec((tc * nb, dw), lambda d, c: (chunk(d, c), 0)),
                  pl.BlockSpec((None, dw, 2 * S5_LANES), lambda d, c: (d, 0, 0)),
                  pl.BlockSpec((None, 2 * S5_LANES, dw), lambda d, c: (d, 0, 0)),
                  pl.BlockSpec((None, 2, S5_LANES), lambda d, c: (d, 0, 0))],
        out_specs=pl.BlockSpec((None, tc * nb, dw), lambda d, c: (d, chunk(d, c), 0)),
        out_shape=jax.ShapeDtypeStruct((2, rows, dw), F32),
        scratch_shapes=[pltpu.VMEM((tc * nb, 2 * S5_LANES), F32),
                        pltpu.VMEM((tc * nb, 2 * S5_LANES), F32),
                        pltpu.VMEM((nb, 2 * S5_LANES), F32)],
        compiler_params=_cparams(("arbitrary", "arbitrary")),
        name="s5_scan",
    )(u2, bmat, cmat, lam)


def _s5_glu_body(y_ref, u_ref, dskip_ref, w_ref, b_ref, o_ref):
    y = y_ref[0] + y_ref[1] + u_ref[...] * dskip_ref[...]
    z = jax.nn.gelu(y)
    gate = jax.nn.sigmoid(_dot(z.astype(BF16), w_ref[...]) + b_ref[...])
    o_ref[...] = (z * gate).astype(o_ref.dtype)


def _s5_glu(y, u2, dskip, w, bias, *, tr):
    rows, dw = u2.shape
    return pl.pallas_call(
        _s5_glu_body,
        grid=(rows // tr,),
        in_specs=[pl.BlockSpec((2, tr, dw), lambda i: (0, i, 0)),
                  pl.BlockSpec((tr, dw), lambda i: (i, 0)),
                  _resident((1, dw)), _resident(w.shape), _resident((1, dw))],
        out_specs=pl.BlockSpec((tr, dw), lambda i: (i, 0)),
        out_shape=jax.ShapeDtypeStruct((rows, dw), BF16),
        compiler_params=_cparams(("arbitrary",)),
        name="s5_glu",
    )(y, u2, dskip, w, bias)


def _rotary_tables(positions, rot_dim, lane_of_first, period):
    half = rot_dim // 2
    inv_freq = ROPE_THETA ** (-jnp.arange(0, rot_dim, 2, dtype=F32) / rot_dim)
    ang = positions.astype(F32)[..., None] * inv_freq
    ang = jnp.concatenate([ang, ang], axis=-1)
    cos_r, sin_r = jnp.cos(ang), jnp.sin(ang)
    lane = np.arange(LANES)
    j = (lane - lane_of_first) % period
    in_rot = (lane >= lane_of_first) & (j < rot_dim)
    idx = np.where(in_rot, j, 0)
    cos = jnp.where(in_rot, cos_r[..., idx], 1.0)
    sin = jnp.where(in_rot, sin_r[..., idx], 0.0)
    s_up = jnp.where(in_rot & (j < half), -sin, 0.0)
    s_dn = jnp.where(in_rot & (j >= half), sin, 0.0)
    return cos, s_up, s_dn


def _dft_mats(n):
    ang = 2.0 * np.pi * np.outer(np.arange(n), np.arange(n)) / n
    return np.cos(ang), np.sin(ang)


def _block_diag(blocks):
    g, r, c = blocks.shape
    eye = jnp.eye(g, dtype=blocks.dtype)
    return jnp.einsum("grc,gh->grhc", blocks, eye).reshape(g * r, g * c)


def _pick_tile(n, pref):
    t = min(n, pref)
    while n % t:
        t //= 2
    return t


def kernel(x, positions, ffn1_norm, ffn1_w_gate, ffn1_w_up, ffn1_w_down, mix_norm, ffn2_norm, ffn2_w_gate, ffn2_w_up, ffn2_w_down, ab_w_in, ab_w_out, a_q_norm, a_k_norm, b_w_mix, cd_w_in, cd_w_out, c_q_lat_norm, c_w_q_up, c_kv_lat_norm, c_w_kv_up, c_q_norm, c_k_norm, d_lam_re, d_lam_im, d_log_step, d_b_re, d_b_im, d_c_re, d_c_im, d_skip, d_w_glu, d_b_glu):
    bsz, seq, d_model = x.shape
    depth = ffn1_norm.shape[0]
    tokens = bsz * seq
    ts = _pick_tile(seq, 512)
    tm = _pick_tile(tokens, 512)
    n1 = seq // FFT_N2
    row = lambda v: v.reshape(1, -1).astype(F32)

    cos_a, sup_a, sdn_a = _rotary_tables(positions, ROT_DIM_A, 0, HEAD_DIM)
    cos_c, sup_c, sdn_c = _rotary_tables(positions, C_ROPE_DIM, C_NOPE_DIM, LANES)

    cc, sc_ = _dft_mats(FNET_GROUP_DIM)
    norm = 1.0 / math.sqrt(seq * FNET_GROUP_DIM)
    eye_g = np.eye(FNET_GROUPS)
    dcos = jnp.asarray(np.kron(eye_g, cc) * norm, BF16)
    dsin = jnp.asarray(np.kron(eye_g, -sc_) * norm, BF16)
    c1, s1 = _dft_mats(n1)
    f1 = jnp.asarray(np.block([[c1, s1], [-s1, c1]]), BF16)
    c2, s2 = _dft_mats(FFT_N2)
    f2 = jnp.asarray(np.concatenate([c2, s2], axis=1), BF16)
    tw_ang = 2.0 * np.pi * np.outer(np.arange(n1), np.arange(FFT_N2)) / seq
    twc = jnp.repeat(jnp.asarray(np.cos(tw_ang), F32), B_WIDTH, axis=1)
    tws = jnp.repeat(jnp.asarray(np.sin(tw_ang), F32), B_WIDTH, axis=1)
    head_mean = jnp.asarray(np.kron(np.eye(N_HEADS_A), np.full((HEAD_DIM, HEAD_DIM), 1.0 / HEAD_DIM)), BF16)

    x2 = x.reshape(tokens, d_model)
    for layer in range(depth):
        i = layer // 2
        x2 = _ffn(x2, row(ffn1_norm[layer]), ffn1_w_gate[layer].astype(BF16),
                  ffn1_w_up[layer].astype(BF16), ffn1_w_down[layer].astype(BF16), tm=tm)
        x3 = x2.reshape(bsz, seq, d_model)
        if layer % 2 == 0:
            mbd = _block_diag(b_w_mix[i]).astype(BF16)
            q, k, v, zr, zi = _even_in(
                x3, row(mix_norm[layer]), ab_w_in[i].astype(BF16), head_mean,
                row(jnp.tile(a_q_norm[i], N_HEADS_A)), row(jnp.tile(a_k_norm[i], N_HEADS_A)),
                cos_a, sup_a, sdn_a, dcos, dsin, mbd, ts=ts)
            a_out = _dilated(q, k, v)
            shp = (bsz, n1, FFT_N2 * B_WIDTH)
            yr, yi = _fft1(zr.reshape(shp), zi.reshape(shp), f1, twc, tws,
                           wb=_pick_tile(FFT_N2 * B_WIDTH, 4096))
            shp2 = (bsz, n1 * FFT_N2, B_WIDTH)
            spec = _fft2(yr.reshape(shp2), yi.reshape(shp2), f2, kb=_pick_tile(n1, 8))
            b_out = spec.reshape(bsz, seq, B_WIDTH)
            w_out = ab_w_out[i].astype(BF16)
            x3 = _out_proj(x3, a_out, b_out, w_out[:A_WIDTH], w_out[A_WIDTH:], ts=ts,
                           b_spec=pl.BlockSpec((None, ts, B_WIDTH), lambda bi, j: (bi, j, 0)))
        else:
            w_in = cd_w_in[i]
            pe_off = C_Q_RANK + C_KV_RANK
            k_pe_w = jnp.pad(w_in[:, pe_off:pe_off + C_ROPE_DIM],
                             ((0, 0), (C_NOPE_DIM, LANES - C_QK_DIM)))
            w_in_p = jnp.concatenate([w_in[:, :pe_off], k_pe_w, w_in[:, pe_off + C_ROPE_DIM:]],
                                     axis=1).astype(BF16)
            pad_h = lambda w, keep: jnp.pad(
                w, ((0, 0), (0, 0), (0, LANES - keep))).reshape(w.shape[0], N_HEADS_C * LANES)
            wq_p = pad_h(c_w_q_up[i].reshape(C_Q_RANK, N_HEADS_C, C_QK_DIM), C_QK_DIM).astype(BF16)
            wkv = c_w_kv_up[i].reshape(C_KV_RANK, N_HEADS_C, C_NOPE_DIM + C_V_DIM)
            wkv_p = jnp.concatenate(
                [pad_h(wkv[:, :, :C_NOPE_DIM], C_NOPE_DIM),
                 wkv[:, :, C_NOPE_DIM:].reshape(C_KV_RANK, N_HEADS_C * C_V_DIM)], axis=1).astype(BF16)
            pad_g = lambda gvec: jnp.pad(gvec, (0, LANES - C_QK_DIM)).reshape(1, LANES).astype(F32)
            q, k, v, u_t = _odd_in(
                x3, row(mix_norm[layer]), w_in_p, row(c_q_lat_norm[i]), wq_p,
                row(c_kv_lat_norm[i]), wkv_p, pad_g(c_q_norm[i]), pad_g(c_k_norm[i]),
                cos_c, sup_c, sdn_c, ts=ts)
            c_out = _flash(q, k, jnp.swapaxes(v, 1, 2), tq=_pick_tile(seq, 256),
                           tk=_pick_tile(seq, 512))

            lam = lax.complex(d_lam_re[i].astype(F32), d_lam_im[i].astype(F32))
            step = jnp.exp(d_log_step[i].astype(F32))[..., None]
            lam_bar = jnp.exp(lam * step)
            b_bar = ((lam_bar - 1.0) / lam)[..., None] * lax.complex(
                d_b_re[i].astype(F32), d_b_im[i].astype(F32))
            to_bd = lambda m: jnp.stack([_block_diag(m[d]) for d in range(2)])
            b_t = jnp.swapaxes(b_bar, -1, -2)
            bmat = jnp.concatenate([to_bd(jnp.real(b_t)), to_bd(jnp.imag(b_t))], axis=-1).astype(BF16)
            c_re_t = jnp.swapaxes(d_c_re[i].astype(F32), -1, -2)
            c_im_t = jnp.swapaxes(d_c_im[i].astype(F32), -1, -2)
            cmat = jnp.concatenate([to_bd(c_re_t), to_bd(-c_im_t)], axis=-2).astype(BF16)
            lam2 = jnp.stack([jnp.real(lam_bar).reshape(2, S5_LANES),
                              jnp.imag(lam_bar).reshape(2, S5_LANES)], axis=1)

            u2 = u_t.reshape(seq * bsz, D_WIDTH)
            tc = _pick_tile(seq, 128)
            y = _s5_scan(u2, bmat, cmat, lam2, tc=tc, nb=bsz)
            d2 = _s5_glu(y, u2, row(d_skip[i]), d_w_glu[i].astype(BF16), row(d_b_glu[i]),
                         tr=_pick_tile(seq * bsz, 1024))
            d_out = d2.reshape(seq, bsz * D_WIDTH)
            w_out = cd_w_out[i].astype(BF16)
            x3 = _out_proj(x3, c_out, d_out, w_out[:N_HEADS_C * C_V_DIM],
                           w_out[N_HEADS_C * C_V_DIM:], ts=ts,
                           b_spec=pl.BlockSpec((ts, D_WIDTH), lambda bi, j: (j, bi)))
        x2 = x3.reshape(tokens, d_model)
        x2 = _ffn(x2, row(ffn2_norm[layer]), ffn2_w_gate[layer].astype(BF16),
                  ffn2_w_up[layer].astype(BF16), ffn2_w_down[layer].astype(BF16), tm=tm)
    return x2.reshape(bsz, seq, d_model)
```

```python
import functools
import math

import jax
import jax.numpy as jnp
import numpy as np
from jax import lax
from jax.experimental import pallas as pl
from jax.experimental.pallas import tpu as pltpu

F32 = jnp.float32
BF16 = jnp.bfloat16

HEAD_DIM = 64
N_HEADS_A = 8
ROT_DIM_A = 16
DILATIONS = (1, 4, 16)
RADIUS = 64
FNET_GROUPS = 4
FNET_GROUP_DIM = 64
N_HEADS_C = 8
C_NOPE_DIM = 64
C_ROPE_DIM = 32
C_V_DIM = 64
C_QK_DIM = C_NOPE_DIM + C_ROPE_DIM
C_Q_RANK = 256
C_KV_RANK = 128
S5_GROUPS = 16
S5_GROUP_DIM = 16
S5_STATE = 64
ROPE_THETA = 500000.0
NORM_EPS = 1e-6
NEG_INF = -1e30
A_WIDTH = N_HEADS_A * HEAD_DIM
B_WIDTH = FNET_GROUPS * FNET_GROUP_DIM
D_WIDTH = S5_GROUPS * S5_GROUP_DIM
S5_LANES = S5_GROUPS * S5_STATE

LANES = 128
SUBLANES = 8
VMEM_LIMIT_BYTES = 56 * 1024 * 1024


def _cparams(sem):
    return pltpu.CompilerParams(dimension_semantics=sem, vmem_limit_bytes=VMEM_LIMIT_BYTES)


def _resident(shape):
    nd = len(shape)
    return pl.BlockSpec(shape, lambda *_: (0,) * nd, pipeline_mode=pl.Buffered(1))


def _rms_rows(x, g):
    ms = jnp.mean(x * x, axis=-1, keepdims=True)
    return x * lax.rsqrt(ms + NORM_EPS) * g


def _dot(a, b):
    return jnp.dot(a, b, preferred_element_type=F32)


FFN_CHUNK = 256


def _ffn_body(*refs, n_chunks, fused_mixer):
    if fused_mixer:
        x_ref, a_ref, b_ref, wa_ref, wb_ref, g_ref, wg_ref, wu_ref, wd_ref, o_ref = refs
        x = x_ref[...] + _dot(a_ref[...], wa_ref[...]) + _dot(b_ref[...], wb_ref[...])
    else:
        x_ref, g_ref, wg_ref, wu_ref, wd_ref, o_ref = refs
        x = x_ref[...]
    h = _rms_rows(x, g_ref[...]).astype(BF16)
    acc = jnp.zeros(x.shape, F32)
    for c in range(n_chunks):
        sl = pl.ds(c * FFN_CHUNK, FFN_CHUNK)
        gate = _dot(h, wg_ref[:, sl])
        up = _dot(h, wu_ref[:, sl])
        act = (gate * jax.nn.sigmoid(gate) * up).astype(BF16)
        acc = acc + _dot(act, wd_ref[sl, :])
    o_ref[...] = x + 0.5 * acc


def _ffn(x2, g, wg, wu, wd, *, tm, mixer=None):
    t, d = x2.shape
    f = wg.shape[1]
    assert t % tm == 0 and f % FFN_CHUNK == 0
    rows = lambda w: pl.BlockSpec((tm, w), lambda i: (i, 0))
    weights = [_resident((1, d)), _resident((d, f)), _resident((d, f)), _resident((f, d))]
    if mixer is None:
        in_specs, args = [rows(d)] + weights, (x2, g, wg, wu, wd)
    else:
        a, b, wa, wb, b_spec = mixer
        in_specs = [rows(d), rows(a.shape[1]), b_spec, _resident(wa.shape), _resident(wb.shape)] + weights
        args = (x2, a, b, wa, wb, g, wg, wu, wd)
    return pl.pallas_call(
        functools.partial(_ffn_body, n_chunks=f // FFN_CHUNK, fused_mixer=mixer is not None),
        grid=(t // tm,),
        in_specs=in_specs,
        out_specs=rows(d),
        out_shape=jax.ShapeDtypeStruct((t, d), F32),
        compiler_params=_cparams(("arbitrary",)),
        name="ffn_mix" if mixer is not None else "ffn",
    )(*args)


def _rotary_block(t, cos, s_up, s_dn, shift):
    return (t * cos + pltpu.roll(t, LANES - shift, axis=1) * s_up
            + pltpu.roll(t, shift, axis=1) * s_dn)


def _even_in_body(x_ref, g_ref, w_ref, pm_ref, gq_ref, gk_ref, cos_ref, sup_ref, sdn_ref,
                  dcos_ref, dsin_ref, mbd_ref,
                  q_ref, k_ref, v_ref, zr_ref, zi_ref, wc_scr):
    first = jnp.logical_and(pl.program_id(0) == 0, pl.program_id(1) == 0)

    @pl.when(first)
    def _():
        mbd = mbd_ref[...]
        wc_scr[:, :B_WIDTH] = _dot(dcos_ref[...], mbd).astype(BF16)
        wc_scr[:, B_WIDTH:] = _dot(dsin_ref[...], mbd).astype(BF16)

    h = _rms_rows(x_ref[...], g_ref[...]).astype(BF16)
    z = _dot(h, w_ref[...])
    cos, s_up, s_dn = cos_ref[...], sup_ref[...], sdn_ref[...]
    pm = pm_ref[...]

    def head_norm_rot(t, gain, scale):
        ms = _dot((t * t).astype(BF16), pm)
        tn = t * lax.rsqrt(ms + NORM_EPS) * gain
        blocks = [_rotary_block(tn[:, c * LANES:(c + 1) * LANES], cos, s_up, s_dn, ROT_DIM_A // 2)
                  for c in range(A_WIDTH // LANES)]
        out = jnp.concatenate(blocks, axis=1)
        return out * scale if scale != 1.0 else out

    q_ref[...] = head_norm_rot(z[:, :A_WIDTH], gq_ref[...], HEAD_DIM ** -0.5 * math.log2(math.e))
    k_ref[...] = head_norm_rot(z[:, A_WIDTH:2 * A_WIDTH], gk_ref[...], 1.0)
    v_ref[...] = z[:, 2 * A_WIDTH:3 * A_WIDTH]
    zz = _dot(z[:, 3 * A_WIDTH:].astype(BF16), wc_scr[...])
    zr_ref[...] = zz[:, :B_WIDTH].astype(BF16)
    zi_ref[...] = zz[:, B_WIDTH:].astype(BF16)


def _even_in(x, g, w_in, pm, gq, gk, cos, s_up, s_dn, dcos, dsin, mbd, *, ts):
    b, s, d = x.shape
    tok = lambda w: pl.BlockSpec((None, ts, w), lambda bi, i: (bi, i, 0))
    return pl.pallas_call(
        _even_in_body,
        grid=(b, s // ts),
        in_specs=[tok(d), _resident((1, d)), _resident(w_in.shape), _resident(pm.shape),
                  _resident((1, A_WIDTH)), _resident((1, A_WIDTH)),
                  tok(LANES), tok(LANES), tok(LANES),
                  _resident(dcos.shape), _resident(dsin.shape), _resident(mbd.shape)],
        out_specs=[tok(A_WIDTH), tok(A_WIDTH), tok(A_WIDTH), tok(B_WIDTH), tok(B_WIDTH)],
        out_shape=[jax.ShapeDtypeStruct((b, s, A_WIDTH), F32)] * 3
        + [jax.ShapeDtypeStruct((b, s, B_WIDTH), BF16)] * 2,
        scratch_shapes=[pltpu.VMEM((B_WIDTH, 2 * B_WIDTH), BF16)],
        compiler_params=_cparams(("arbitrary", "arbitrary")),
        name="even_in",
    )(x, g, w_in, pm, gq, gk, cos, s_up, s_dn, dcos, dsin, mbd)


DIL_TQ = 128
DIL_TK = DIL_TQ + 2 * RADIUS


DIL_UNROLL = 4
DIL_COPY_ROWS = 512


def _dilated_body(q_ref, k_ref, v_ref, o_ref, qd, kd, vd, acc_a, acc_b, m_a, m_b, bias_scr, *, seq):
    lane = lax.broadcasted_iota(jnp.int32, (1, LANES), 1)
    is_a = lane < HEAD_DIM
    qq = lax.broadcasted_iota(jnp.int32, (DIL_TQ, DIL_TK), 0)
    kk = lax.broadcasted_iota(jnp.int32, (DIL_TQ, DIL_TK), 1)
    for idx in range(3):
        bias_scr[idx] = jnp.where(jnp.abs(kk - qq - RADIUS * idx) <= RADIUS, 0.0, NEG_INF)

    order = tuple(reversed(DILATIONS))
    for dil in order:
        first, last = dil == order[0], dil == order[-1]
        sub_len = seq // dil
        n_tiles = sub_len // DIL_TQ
        copy_rows = min(DIL_COPY_ROWS, sub_len)
        chunks = sub_len // copy_rows

        def rows(res, first_row, n, dil=dil):
            if dil == 1:
                return pl.ds(pl.multiple_of(first_row, DIL_TQ), n)
            return pl.ds(res + dil * first_row, n, stride=dil)

        def deinterleave(idx, carry, dil=dil, sub_len=sub_len, rows=rows, copy_rows=copy_rows,
                         chunks=chunks):
            res = idx // chunks
            c0 = (idx - res * chunks) * copy_rows
            src = rows(res, c0, copy_rows)
            dst = pl.ds(pl.multiple_of(res * sub_len + c0, copy_rows), copy_rows)
            qd[dst, :] = q_ref[src, :].astype(BF16)
            kd[dst, :] = k_ref[src, :].astype(BF16)
            vd[dst, :] = v_ref[src, :].astype(BF16)
            return carry

        lax.fori_loop(0, dil * chunks, deinterleave, 0)

        groups = n_tiles // DIL_UNROLL

        def tiles(idx, carry, dil=dil, sub_len=sub_len, rows=rows, groups=groups,
                  first=first, last=last):
            res = idx // groups
            g = idx - res * groups
            base = res * sub_len
            loaded = []
            for u in range(DIL_UNROLL):
                q0 = (g * DIL_UNROLL + u) * DIL_TQ
                k0 = jnp.clip(q0 - RADIUS, 0, sub_len - DIL_TK)
                k_rows = pl.ds(pl.multiple_of(base + k0, RADIUS), DIL_TK)
                st_rows = rows(res, q0, DIL_TQ)
                item = dict(
                    st_rows=st_rows,
                    q2=qd[pl.ds(pl.multiple_of(base + q0, DIL_TQ), DIL_TQ), :],
                    k2=kd[k_rows, :], v2=vd[k_rows, :],
                    bias=bias_scr[lax.shift_right_logical(q0 - k0, RADIUS.bit_length() - 1)])
                if not first:
                    item.update(acc=(acc_a[st_rows, :], acc_b[st_rows, :]),
                                m=(m_a[st_rows, :], m_b[st_rows, :]))
                loaded.append(item)
            results = []
            for item in loaded:
                new_acc, new_m = [], []
                for h in range(2):
                    sel = is_a if h == 0 else jnp.logical_not(is_a)
                    qh = jnp.where(sel, item["q2"], jnp.zeros_like(item["q2"]))
                    vh = jnp.where(sel, item["v2"], jnp.ones_like(item["v2"]))
                    sc = lax.dot_general(qh, item["k2"], (((1,), (1,)), ((), ())),
                                         preferred_element_type=F32) + item["bias"]
                    row_max = jnp.max(sc, axis=1, keepdims=True)
                    if first:
                        m_new = jnp.broadcast_to(row_max, (DIL_TQ, LANES))
                        acc = _dot(jnp.exp2(sc - row_max).astype(BF16), vh)
                    else:
                        m_old = item["m"][h]
                        m_new = jnp.maximum(m_old, row_max)
                        p = jnp.exp2(sc - m_new[:, :1]).astype(BF16)
                        acc = item["acc"][h] * jnp.exp2(m_old - m_new) + _dot(p, vh)
                    new_acc.append(acc)
                    new_m.append(m_new)
                results.append((item["st_rows"], new_acc, new_m))
            for st_rows, new_acc, new_m in results:
                if last:
                    out_a = new_acc[0] / pltpu.roll(new_acc[0], HEAD_DIM, axis=1)
                    out_b = new_acc[1] / pltpu.roll(new_acc[1], HEAD_DIM, axis=1)
                    o_ref[st_rows, :] = jnp.where(is_a, out_a, out_b).astype(o_ref.dtype)
                else:
                    acc_a[st_rows, :] = new_acc[0]
                    acc_b[st_rows, :] = new_acc[1]
                    m_a[st_rows, :] = new_m[0]
                    m_b[st_rows, :] = new_m[1]
            return carry

        lax.fori_loop(0, dil * groups, tiles, 0)


def _dilated(q, k, v):
    b, s, w = q.shape
    assert DILATIONS[0] == 1
    assert s % (DILATIONS[-1] * DIL_TQ * DIL_UNROLL) == 0 and s // DILATIONS[-1] >= DIL_TK
    spec = pl.BlockSpec((None, s, LANES), lambda bi, p: (bi, 0, p))
    return pl.pallas_call(
        functools.partial(_dilated_body, seq=s),
        grid=(b, w // LANES),
        in_specs=[pl.BlockSpec((None, s, LANES), lambda bi, p: (bi, 0, p),
                               pipeline_mode=pl.Buffered(1))] * 3,
        out_specs=spec,
        out_shape=jax.ShapeDtypeStruct((b, s, w), BF16),
        scratch_shapes=[pltpu.VMEM((s, LANES), BF16)] * 3 + [pltpu.VMEM((s, LANES), F32)] * 4
        + [pltpu.VMEM((3, DIL_TQ, DIL_TK), F32)],
        compiler_params=_cparams(("arbitrary", "arbitrary")),
        name="dilated",
    )(q, k, v)


FFT_N2 = 128


def _fft1_body(zr_ref, zi_ref, f_ref, twc_ref, tws_ref, yr_ref, yi_ref, *, n1):
    zz = jnp.concatenate([zr_ref[...], zi_ref[...]], axis=0)
    y = _dot(f_ref[...], zz)
    yr, yi = y[:n1], y[n1:]
    c, s = twc_ref[...], tws_ref[...]
    yr_ref[...] = (yr * c + yi * s).astype(BF16)
    yi_ref[...] = (yi * c - yr * s).astype(BF16)


def _fft1(zr, zi, fmat, twc, tws, *, wb):
    b, n1, w = zr.shape
    dat = pl.BlockSpec((None, n1, wb), lambda bi, j: (bi, 0, j))
    tw = pl.BlockSpec((n1, wb), lambda bi, j: (0, j))
    return pl.pallas_call(
        functools.partial(_fft1_body, n1=n1),
        grid=(b, w // wb),
        in_specs=[dat, dat, _resident(fmat.shape), tw, tw],
        out_specs=[dat, dat],
        out_shape=[jax.ShapeDtypeStruct((b, n1, w), BF16)] * 2,
        compiler_params=_cparams(("arbitrary", "arbitrary")),
        name="fft_stage1",
    )(zr, zi, fmat, twc, tws)


def _fft2_body(yr_ref, yi_ref, f_ref, o_ref, *, kb):
    f = f_ref[...]
    for j in range(kb):
        rs = pl.ds(j * FFT_N2, FFT_N2)
        yy = jnp.concatenate([yr_ref[rs, :], yi_ref[rs, :]], axis=0)
        o_ref[:, j * B_WIDTH:(j + 1) * B_WIDTH] = _dot(f, yy).astype(o_ref.dtype)


def _fft2(yr, yi, fmat, *, kb):
    b, rows, w = yr.shape
    n1 = rows // FFT_N2
    dat = pl.BlockSpec((None, kb * FFT_N2, w), lambda bi, j: (bi, j, 0))
    return pl.pallas_call(
        functools.partial(_fft2_body, kb=kb),
        grid=(b, n1 // kb),
        in_specs=[dat, dat, _resident(fmat.shape)],
        out_specs=pl.BlockSpec((None, FFT_N2, kb * w), lambda bi, j: (bi, 0, j)),
        out_shape=jax.ShapeDtypeStruct((b, FFT_N2, n1 * w), BF16),
        compiler_params=_cparams(("arbitrary", "arbitrary")),
        name="fft_stage2",
    )(yr, yi, fmat)


def _odd_in_body(x_ref, g_ref, w_ref, gql_ref, wq_ref, gkvl_ref, wkv_ref, gq_ref, gk_ref,
                 cos_ref, sup_ref, sdn_ref, q_ref, k_ref, v_ref, u_ref):
    h = _rms_rows(x_ref[...], g_ref[...]).astype(BF16)
    z = _dot(h, w_ref[...])
    cos, s_up, s_dn = cos_ref[...], sup_ref[...], sdn_ref[...]
    kv_off = C_Q_RANK
    pe_off = kv_off + C_KV_RANK
    u_off = pe_off + LANES
    q_lat = _rms_rows(z[:, :C_Q_RANK], gql_ref[...]).astype(BF16)
    kv_lat = _rms_rows(z[:, kv_off:pe_off], gkvl_ref[...]).astype(BF16)
    k_pe = z[:, pe_off:u_off]
    qf = _dot(q_lat, wq_ref[...])
    kvf = _dot(kv_lat, wkv_ref[...])
    gq, gk = gq_ref[...], gk_ref[...]

    def head(t, gain, scale):
        ms = jnp.sum(t * t, axis=-1, keepdims=True) * (1.0 / C_QK_DIM)
        tn = t * lax.rsqrt(ms + NORM_EPS) * gain
        out = _rotary_block(tn, cos, s_up, s_dn, C_ROPE_DIM // 2)
        return out * scale if scale != 1.0 else out

    for hd in range(N_HEADS_C):
        sl = slice(hd * LANES, (hd + 1) * LANES)
        q_ref[:, sl] = head(qf[:, sl], gq, C_QK_DIM ** -0.5 * math.log2(math.e)).astype(BF16)
        k_ref[:, sl] = head(kvf[:, sl] + k_pe, gk, 1.0).astype(BF16)
    v_ref[...] = kvf[:, N_HEADS_C * LANES:].astype(BF16)
    u_ref[...] = z[:, u_off:]


def _odd_in(x, g, w_in, gql, wq, gkvl, wkv, gq, gk, cos, s_up, s_dn, *, ts):
    b, s, d = x.shape
    tok = lambda w: pl.BlockSpec((None, ts, w), lambda bi, i: (bi, i, 0))
    hw = N_HEADS_C * LANES
    return pl.pallas_call(
        _odd_in_body,
        grid=(b, s // ts),
        in_specs=[tok(d), _resident((1, d)), _resident(w_in.shape), _resident(gql.shape),
                  _resident(wq.shape), _resident(gkvl.shape), _resident(wkv.shape),
                  _resident(gq.shape), _resident(gk.shape), tok(LANES), tok(LANES), tok(LANES)],
        out_specs=[tok(hw), tok(hw), tok(N_HEADS_C * C_V_DIM),
                   pl.BlockSpec((ts, D_WIDTH), lambda bi, i: (i, bi))],
        out_shape=[jax.ShapeDtypeStruct((b, s, hw), BF16), jax.ShapeDtypeStruct((b, s, hw), BF16),
                   jax.ShapeDtypeStruct((b, s, N_HEADS_C * C_V_DIM), BF16),
                   jax.ShapeDtypeStruct((s, b * D_WIDTH), F32)],
        compiler_params=_cparams(("arbitrary", "arbitrary")),
        name="odd_in",
    )(x, g, w_in, gql, wq, gkvl, wkv, gq, gk, cos, s_up, s_dn)


FLASH_ONES_ROWS = 16


def _flash_body(q_ref, k_ref, vt_ref, o_ref, s_scr, p_scr, *, tq, tk):
    n_q = q_ref.shape[0] // tq
    n_kv = k_ref.shape[0] // tk
    heads = ((0, 0), (LANES, C_V_DIM))
    ones = jnp.ones((FLASH_ONES_ROWS, tk), BF16)
    items = [(qi, kj) for qi in range(n_q) for kj in range(n_kv)]

    def scores(item, slot):
        qi, kj = item
        col_max = []
        for h, (off, _) in enumerate(heads):
            s = lax.dot_general(k_ref[kj * tk:(kj + 1) * tk, off:off + LANES],
                                q_ref[qi * tq:(qi + 1) * tq, off:off + LANES],
                                (((1,), (1,)), ((), ())), preferred_element_type=F32)
            s_scr[slot, h] = s
            col_max.append(jnp.max(s, axis=0, keepdims=True))
        return col_max

    def softmax(slot, m2, col_max2):
        m_out, alpha_out = [], []
        for h in range(2):
            m_new = jnp.maximum(m2[h], col_max2[h])
            p_scr[slot, h] = jnp.exp2(s_scr[slot, h] - m_new).astype(BF16)
            alpha_out.append(jnp.exp2(m2[h] - m_new))
            m_out.append(m_new)
        return m_out, alpha_out

    def accumulate(item, slot, acc2, alpha2):
        _, kj = item
        out = []
        for h, (_, voff) in enumerate(heads):
            vt = jnp.concatenate([vt_ref[voff:voff + C_V_DIM, kj * tk:(kj + 1) * tk], ones], axis=0)
            out.append(acc2[h] * alpha2[h] + _dot(vt, p_scr[slot, h]))
        return out

    m0 = jnp.full((1, tq), NEG_INF, F32)
    a0 = jnp.zeros((C_V_DIM + FLASH_ONES_ROWS, tq), F32)
    m_state = [[m0, m0] for _ in range(n_q)]
    acc_state = [[a0, a0] for _ in range(n_q)]
    col, alpha = {}, {}
    for t in range(len(items) + 2):
        if t < len(items):
            col[t] = scores(items[t], t % 2)
        if 0 <= t - 1 < len(items):
            qi = items[t - 1][0]
            m_state[qi], alpha[t - 1] = softmax((t - 1) % 2, m_state[qi], col.pop(t - 1))
        if 0 <= t - 2 < len(items):
            qi, kj = items[t - 2]
            acc_state[qi] = accumulate(items[t - 2], t % 2, acc_state[qi], alpha.pop(t - 2))
            if kj == n_kv - 1:
                out_t = jnp.concatenate(
                    [acc[:C_V_DIM] / acc[C_V_DIM:C_V_DIM + 1] for acc in acc_state[qi]], axis=0)
                o_ref[qi * tq:(qi + 1) * tq, :] = out_t.T.astype(o_ref.dtype)


def _flash(q, k, vt, *, tq, tk, n_q):
    b, s, hw = q.shape
    pairs = hw // (2 * LANES)
    bq = tq * n_q
    assert s % bq == 0 and s % tk == 0
    return pl.pallas_call(
        functools.partial(_flash_body, tq=tq, tk=tk),
        grid=(b, pairs, s // bq),
        in_specs=[pl.BlockSpec((None, bq, 2 * LANES), lambda bi, p, i: (bi, i, p)),
                  pl.BlockSpec((None, s, 2 * LANES), lambda bi, p, i: (bi, 0, p)),
                  pl.BlockSpec((None, 2 * C_V_DIM, s), lambda bi, p, i: (bi, p, 0))],
        out_specs=pl.BlockSpec((None, bq, LANES), lambda bi, p, i: (bi, i, p)),
        out_shape=jax.ShapeDtypeStruct((b, s, pairs * LANES), BF16),
        scratch_shapes=[pltpu.VMEM((2, 2, tk, tq), F32), pltpu.VMEM((2, 2, tk, tq), BF16)],
        compiler_params=_cparams(("arbitrary", "arbitrary", "arbitrary")),
        name="flash",
    )(q, k, vt)


def _s5_body(u_ref, bm_ref, cm_ref, lam_ref, y_ref, bu_scr, xs_scr, st_scr, *, tc, nb):
    direction = pl.program_id(0)

    @pl.when(pl.program_id(1) == 0)
    def _():
        st_scr[...] = jnp.zeros(st_scr.shape, F32)

    bu_scr[...] = _dot(u_ref[...].astype(BF16), bm_ref[...])
    a_re = jnp.broadcast_to(lam_ref[0:1, :], (nb, S5_LANES))
    a_im = jnp.broadcast_to(lam_ref[1:2, :], (nb, S5_LANES))

    def step(t, carry):
        x_re, x_im = carry
        tt = jnp.where(direction == 0, t, tc - 1 - t)
        rs = pl.ds(pl.multiple_of(tt * nb, nb), nb)
        n_re = a_re * x_re - a_im * x_im + bu_scr[rs, :S5_LANES]
        n_im = a_re * x_im + a_im * x_re + bu_scr[rs, S5_LANES:]
        xs_scr[rs, :S5_LANES] = n_re
        xs_scr[rs, S5_LANES:] = n_im
        return n_re, n_im

    x_re, x_im = lax.fori_loop(0, tc, step, (st_scr[:, :S5_LANES], st_scr[:, S5_LANES:]),
                               unroll=4)
    st_scr[:, :S5_LANES] = x_re
    st_scr[:, S5_LANES:] = x_im
    y_ref[...] = _dot(xs_scr[...].astype(BF16), cm_ref[...])


def _s5_scan(u2, bmat, cmat, lam, *, tc, nb):
    rows, dw = u2.shape
    n_chunks = rows // (tc * nb)

    def chunk(d, c):
        return jnp.where(d == 0, c, n_chunks - 1 - c)

    return pl.pallas_call(
        functools.partial(_s5_body, tc=tc, nb=nb),
        grid=(2, n_chunks),
        in_specs=[pl.BlockSpec((tc * nb, dw), lambda d, c: (chunk(d, c), 0)),
                  pl.BlockSpec((None, dw, 2 * S5_LANES), lambda d, c: (d, 0, 0)),
                  pl.BlockSpec((None, 2 * S5_LANES, dw), lambda d, c: (d, 0, 0)),
                  pl.BlockSpec((None, 2, S5_LANES), lambda d, c: (d, 0, 0))],
        out_specs=pl.BlockSpec((None, tc * nb, dw), lambda d, c: (d, chunk(d, c), 0)),
        out_shape=jax.ShapeDtypeStruct((2, rows, dw), F32),
        scratch_shapes=[pltpu.VMEM((tc * nb, 2 * S5_LANES), F32),
                        pltpu.VMEM((tc * nb, 2 * S5_LANES), F32),
                        pltpu.VMEM((nb, 2 * S5_LANES), F32)],
        compiler_params=_cparams(("arbitrary", "arbitrary")),
        name="s5_scan",
    )(u2, bmat, cmat, lam)


def _s5_glu_body(y_ref, u_ref, dskip_ref, w_ref, b_ref, o_ref):
    y = y_ref[0] + y_ref[1] + u_ref[...] * dskip_ref[...]
    z = jax.nn.gelu(y)
    gate = jax.nn.sigmoid(_dot(z.astype(BF16), w_ref[...]) + b_ref[...])
    o_ref[...] = (z * gate).astype(o_ref.dtype)


def _s5_glu(y, u2, dskip, w, bias, *, tr):
    rows, dw = u2.shape
    return pl.pallas_call(
        _s5_glu_body,
        grid=(rows // tr,),
        in_specs=[pl.BlockSpec((2, tr, dw), lambda i: (0, i, 0)),
                  pl.BlockSpec((tr, dw), lambda i: (i, 0)),
                  _resident((1, dw)), _resident(w.shape), _resident((1, dw))],
        out_specs=pl.BlockSpec((tr, dw), lambda i: (i, 0)),
        out_shape=jax.ShapeDtypeStruct((rows, dw), BF16),
        compiler_params=_cparams(("arbitrary",)),
        name="s5_glu",
    )(y, u2, dskip, w, bias)


def _rotary_tables(positions, rot_dim, lane_of_first, period):
    half = rot_dim // 2
    inv_freq = ROPE_THETA ** (-jnp.arange(0, rot_dim, 2, dtype=F32) / rot_dim)
    ang = positions.astype(F32)[..., None] * inv_freq
    ang = jnp.concatenate([ang, ang], axis=-1)
    cos_r, sin_r = jnp.cos(ang), jnp.sin(ang)
    lane = np.arange(LANES)
    j = (lane - lane_of_first) % period
    in_rot = (lane >= lane_of_first) & (j < rot_dim)
    idx = np.where(in_rot, j, 0)
    cos = jnp.where(in_rot, cos_r[..., idx], 1.0)
    sin = jnp.where(in_rot, sin_r[..., idx], 0.0)
    s_up = jnp.where(in_rot & (j < half), -sin, 0.0)
    s_dn = jnp.where(in_rot & (j >= half), sin, 0.0)
    return cos, s_up, s_dn


def _dft_mats(n):
    ang = 2.0 * np.pi * np.outer(np.arange(n), np.arange(n)) / n
    return np.cos(ang), np.sin(ang)


def _block_diag(blocks):
    g, r, c = blocks.shape
    eye = jnp.eye(g, dtype=blocks.dtype)
    return jnp.einsum("grc,gh->grhc", blocks, eye).reshape(g * r, g * c)


def _pick_tile(n, pref):
    t = min(n, pref)
    while n % t:
        t //= 2
    return t


def kernel(x, positions, ffn1_norm, ffn1_w_gate, ffn1_w_up, ffn1_w_down, mix_norm, ffn2_norm, ffn2_w_gate, ffn2_w_up, ffn2_w_down, ab_w_in, ab_w_out, a_q_norm, a_k_norm, b_w_mix, cd_w_in, cd_w_out, c_q_lat_norm, c_w_q_up, c_kv_lat_norm, c_w_kv_up, c_q_norm, c_k_norm, d_lam_re, d_lam_im, d_log_step, d_b_re, d_b_im, d_c_re, d_c_im, d_skip, d_w_glu, d_b_glu):
    bsz, seq, d_model = x.shape
    depth = ffn1_norm.shape[0]
    tokens = bsz * seq
    ts = _pick_tile(seq, 512)
    tm = _pick_tile(seq, 512)
    n1 = seq // FFT_N2
    row = lambda v: v.reshape(1, -1).astype(F32)

    cos_a, sup_a, sdn_a = _rotary_tables(positions, ROT_DIM_A, 0, HEAD_DIM)
    cos_c, sup_c, sdn_c = _rotary_tables(positions, C_ROPE_DIM, C_NOPE_DIM, LANES)

    cc, sc_ = _dft_mats(FNET_GROUP_DIM)
    norm = 1.0 / math.sqrt(seq * FNET_GROUP_DIM)
    eye_g = np.eye(FNET_GROUPS)
    dcos = jnp.asarray(np.kron(eye_g, cc) * norm, BF16)
    dsin = jnp.asarray(np.kron(eye_g, -sc_) * norm, BF16)
    c1, s1 = _dft_mats(n1)
    f1 = jnp.asarray(np.block([[c1, s1], [-s1, c1]]), BF16)
    c2, s2 = _dft_mats(FFT_N2)
    f2 = jnp.asarray(np.concatenate([c2, s2], axis=1), BF16)
    tw_ang = 2.0 * np.pi * np.outer(np.arange(n1), np.arange(FFT_N2)) / seq
    twc = jnp.repeat(jnp.asarray(np.cos(tw_ang), F32), B_WIDTH, axis=1)
    tws = jnp.repeat(jnp.asarray(np.sin(tw_ang), F32), B_WIDTH, axis=1)
    head_mean = jnp.asarray(np.kron(np.eye(N_HEADS_A), np.full((HEAD_DIM, HEAD_DIM), 1.0 / HEAD_DIM)), BF16)

    x2 = x.reshape(tokens, d_model)
    for layer in range(depth):
        i = layer // 2
        x2 = _ffn(x2, row(ffn1_norm[layer]), ffn1_w_gate[layer].astype(BF16),
                  ffn1_w_up[layer].astype(BF16), ffn1_w_down[layer].astype(BF16), tm=tm)
        x3 = x2.reshape(bsz, seq, d_model)
        if layer % 2 == 0:
            mbd = _block_diag(b_w_mix[i]).astype(BF16)
            q, k, v, zr, zi = _even_in(
                x3, row(mix_norm[layer]), ab_w_in[i].astype(BF16), head_mean,
                row(jnp.tile(a_q_norm[i], N_HEADS_A)), row(jnp.tile(a_k_norm[i], N_HEADS_A)),
                cos_a, sup_a, sdn_a, dcos, dsin, mbd, ts=ts)
            a_out = _dilated(q, k, v)
            shp = (bsz, n1, FFT_N2 * B_WIDTH)
            yr, yi = _fft1(zr.reshape(shp), zi.reshape(shp), f1, twc, tws,
                           wb=_pick_tile(FFT_N2 * B_WIDTH, 4096))
            shp2 = (bsz, n1 * FFT_N2, B_WIDTH)
            spec = _fft2(yr.reshape(shp2), yi.reshape(shp2), f2, kb=_pick_tile(n1, 8))
            w_out = ab_w_out[i].astype(BF16)
            mixer = (a_out.reshape(tokens, A_WIDTH), spec.reshape(tokens, B_WIDTH),
                     w_out[:A_WIDTH], w_out[A_WIDTH:],
                     pl.BlockSpec((tm, B_WIDTH), lambda r: (r, 0)))
        else:
            w_in = cd_w_in[i]
            pe_off = C_Q_RANK + C_KV_RANK
            k_pe_w = jnp.pad(w_in[:, pe_off:pe_off + C_ROPE_DIM],
                             ((0, 0), (C_NOPE_DIM, LANES - C_QK_DIM)))
            w_in_p = jnp.concatenate([w_in[:, :pe_off], k_pe_w, w_in[:, pe_off + C_ROPE_DIM:]],
                                     axis=1).astype(BF16)
            pad_h = lambda w, keep: jnp.pad(
                w, ((0, 0), (0, 0), (0, LANES - keep))).reshape(w.shape[0], N_HEADS_C * LANES)
            wq_p = pad_h(c_w_q_up[i].reshape(C_Q_RANK, N_HEADS_C, C_QK_DIM), C_QK_DIM).astype(BF16)
            wkv = c_w_kv_up[i].reshape(C_KV_RANK, N_HEADS_C, C_NOPE_DIM + C_V_DIM)
            wkv_p = jnp.concatenate(
                [pad_h(wkv[:, :, :C_NOPE_DIM], C_NOPE_DIM),
                 wkv[:, :, C_NOPE_DIM:].reshape(C_KV_RANK, N_HEADS_C * C_V_DIM)], axis=1).astype(BF16)
            pad_g = lambda gvec: jnp.pad(gvec, (0, LANES - C_QK_DIM)).reshape(1, LANES).astype(F32)
            q, k, v, u_t = _odd_in(
                x3, row(mix_norm[layer]), w_in_p, row(c_q_lat_norm[i]), wq_p,
                row(c_kv_lat_norm[i]), wkv_p, pad_g(c_q_norm[i]), pad_g(c_k_norm[i]),
                cos_c, sup_c, sdn_c, ts=ts)
            c_out = _flash(q, k, jnp.swapaxes(v, 1, 2), tq=_pick_tile(seq, 256),
                           tk=_pick_tile(seq, 512), n_q=2)

            lam = lax.complex(d_lam_re[i].astype(F32), d_lam_im[i].astype(F32))
            step = jnp.exp(d_log_step[i].astype(F32))[..., None]
            lam_bar = jnp.exp(lam * step)
            b_bar = ((lam_bar - 1.0) / lam)[..., None] * lax.complex(
                d_b_re[i].astype(F32), d_b_im[i].astype(F32))
            to_bd = lambda m: jnp.stack([_block_diag(m[d]) for d in range(2)])
            b_t = jnp.swapaxes(b_bar, -1, -2)
            bmat = jnp.concatenate([to_bd(jnp.real(b_t)), to_bd(jnp.imag(b_t))], axis=-1).astype(BF16)
            c_re_t = jnp.swapaxes(d_c_re[i].astype(F32), -1, -2)
            c_im_t = jnp.swapaxes(d_c_im[i].astype(F32), -1, -2)
            cmat = jnp.concatenate([to_bd(c_re_t), to_bd(-c_im_t)], axis=-2).astype(BF16)
            lam2 = jnp.stack([jnp.real(lam_bar).reshape(2, S5_LANES),
                              jnp.imag(lam_bar).reshape(2, S5_LANES)], axis=1)

            u2 = u_t.reshape(seq * bsz, D_WIDTH)
            tc = _pick_tile(seq, 128)
            y = _s5_scan(u2, bmat, cmat, lam2, tc=tc, nb=bsz)
            d2 = _s5_glu(y, u2, row(d_skip[i]), d_w_glu[i].astype(BF16), row(d_b_glu[i]),
                         tr=_pick_tile(seq * bsz, 1024))
            d_out = d2.reshape(seq, bsz * D_WIDTH)
            w_out = cd_w_out[i].astype(BF16)
            tiles_per_seq = seq // tm
            mixer = (c_out.reshape(tokens, N_HEADS_C * C_V_DIM), d_out,
                     w_out[:N_HEADS_C * C_V_DIM], w_out[N_HEADS_C * C_V_DIM:],
                     pl.BlockSpec((tm, D_WIDTH),
                                  lambda r: (r % tiles_per_seq, r // tiles_per_seq)))
        x2 = _ffn(x2, row(ffn2_norm[layer]), ffn2_w_gate[layer].astype(BF16),
                  ffn2_w_up[layer].astype(BF16), ffn2_w_down[layer].astype(BF16), tm=tm,
                  mixer=mixer)
    return x2.reshape(bsz, seq, d_model)
```

```python
import functools
import math

import jax
import jax.numpy as jnp
import numpy as np
from jax import lax
from jax.experimental import pallas as pl
from jax.experimental.pallas import tpu as pltpu

F32 = jnp.float32
BF16 = jnp.bfloat16

HEAD_DIM = 64
N_HEADS_A = 8
ROT_DIM_A = 16
DILATIONS = (1, 4, 16)
RADIUS = 64
FNET_GROUPS = 4
FNET_GROUP_DIM = 64
N_HEADS_C = 8
C_NOPE_DIM = 64
C_ROPE_DIM = 32
C_V_DIM = 64
C_QK_DIM = C_NOPE_DIM + C_ROPE_DIM
C_Q_RANK = 256
C_KV_RANK = 128
S5_GROUPS = 16
S5_GROUP_DIM = 16
S5_STATE = 64
ROPE_THETA = 500000.0
NORM_EPS = 1e-6
NEG_INF = -1e30
A_WIDTH = N_HEADS_A * HEAD_DIM
B_WIDTH = FNET_GROUPS * FNET_GROUP_DIM
D_WIDTH = S5_GROUPS * S5_GROUP_DIM
S5_LANES = S5_GROUPS * S5_STATE

LANES = 128
SUBLANES = 8
VMEM_LIMIT_BYTES = 56 * 1024 * 1024


def _cparams(sem):
    return pltpu.CompilerParams(dimension_semantics=sem, vmem_limit_bytes=VMEM_LIMIT_BYTES)


def _resident(shape):
    nd = len(shape)
    return pl.BlockSpec(shape, lambda *_: (0,) * nd, pipeline_mode=pl.Buffered(1))


def _rms_rows(x, g):
    ms = jnp.mean(x * x, axis=-1, keepdims=True)
    return x * lax.rsqrt(ms + NORM_EPS) * g


def _dot(a, b):
    return jnp.dot(a, b, preferred_element_type=F32)


FFN_CHUNK = 256


def _ffn_body(*refs, n_chunks, fused_mixer):
    if fused_mixer:
        x_ref, a_ref, b_ref, wa_ref, wb_ref, g_ref, wg_ref, wu_ref, wd_ref, o_ref = refs
        x = x_ref[...] + _dot(a_ref[...], wa_ref[...]) + _dot(b_ref[...], wb_ref[...])
    else:
        x_ref, g_ref, wg_ref, wu_ref, wd_ref, o_ref = refs
        x = x_ref[...]
    h = _rms_rows(x, g_ref[...]).astype(BF16)
    acc = jnp.zeros(x.shape, F32)
    for c in range(n_chunks):
        sl = pl.ds(c * FFN_CHUNK, FFN_CHUNK)
        gate = _dot(h, wg_ref[:, sl])
        up = _dot(h, wu_ref[:, sl])
        act = (gate * jax.nn.sigmoid(gate) * up).astype(BF16)
        acc = acc + _dot(act, wd_ref[sl, :])
    o_ref[...] = x + 0.5 * acc


def _ffn(x2, g, wg, wu, wd, *, tm, mixer=None):
    t, d = x2.shape
    f = wg.shape[1]
    assert t % tm == 0 and f % FFN_CHUNK == 0
    rows = lambda w: pl.BlockSpec((tm, w), lambda i: (i, 0))
    weights = [_resident((1, d)), _resident((d, f)), _resident((d, f)), _resident((f, d))]
    if mixer is None:
        in_specs, args = [rows(d)] + weights, (x2, g, wg, wu, wd)
    else:
        a, b, wa, wb, b_spec = mixer
        in_specs = [rows(d), rows(a.shape[1]), b_spec, _resident(wa.shape), _resident(wb.shape)] + weights
        args = (x2, a, b, wa, wb, g, wg, wu, wd)
    return pl.pallas_call(
        functools.partial(_ffn_body, n_chunks=f // FFN_CHUNK, fused_mixer=mixer is not None),
        grid=(t // tm,),
        in_specs=in_specs,
        out_specs=rows(d),
        out_shape=jax.ShapeDtypeStruct((t, d), F32),
        compiler_params=_cparams(("arbitrary",)),
        name="ffn_mix" if mixer is not None else "ffn",
    )(*args)


def _rotary_block(t, cos, s_up, s_dn, shift):
    return (t * cos + pltpu.roll(t, LANES - shift, axis=1) * s_up
            + pltpu.roll(t, shift, axis=1) * s_dn)


def _even_in_body(x_ref, g_ref, w_ref, pm_ref, gq_ref, gk_ref, cos_ref, sup_ref, sdn_ref,
                  dcos_ref, dsin_ref, mbd_ref,
                  q_ref, k_ref, v_ref, zr_ref, zi_ref, wc_scr):
    first = jnp.logical_and(pl.program_id(0) == 0, pl.program_id(1) == 0)

    @pl.when(first)
    def _():
        mbd = mbd_ref[...]
        wc_scr[:, :B_WIDTH] = _dot(dcos_ref[...], mbd).astype(BF16)
        wc_scr[:, B_WIDTH:] = _dot(dsin_ref[...], mbd).astype(BF16)

    h = _rms_rows(x_ref[...], g_ref[...]).astype(BF16)
    z = _dot(h, w_ref[...])
    cos, s_up, s_dn = cos_ref[...], sup_ref[...], sdn_ref[...]
    pm = pm_ref[...]

    def head_norm_rot(t, gain, scale):
        ms = _dot((t * t).astype(BF16), pm)
        tn = t * lax.rsqrt(ms + NORM_EPS) * gain
        blocks = [_rotary_block(tn[:, c * LANES:(c + 1) * LANES], cos, s_up, s_dn, ROT_DIM_A // 2)
                  for c in range(A_WIDTH // LANES)]
        out = jnp.concatenate(blocks, axis=1)
        return out * scale if scale != 1.0 else out

    q_ref[...] = head_norm_rot(z[:, :A_WIDTH], gq_ref[...], HEAD_DIM ** -0.5 * math.log2(math.e))
    k_ref[...] = head_norm_rot(z[:, A_WIDTH:2 * A_WIDTH], gk_ref[...], 1.0)
    v_ref[...] = z[:, 2 * A_WIDTH:3 * A_WIDTH]
    zz = _dot(z[:, 3 * A_WIDTH:].astype(BF16), wc_scr[...])
    zr_ref[...] = zz[:, :B_WIDTH].astype(BF16)
    zi_ref[...] = zz[:, B_WIDTH:].astype(BF16)


def _even_in(x, g, w_in, pm, gq, gk, cos, s_up, s_dn, dcos, dsin, mbd, *, ts):
    b, s, d = x.shape
    tok = lambda w: pl.BlockSpec((None, ts, w), lambda bi, i: (bi, i, 0))
    return pl.pallas_call(
        _even_in_body,
        grid=(b, s // ts),
        in_specs=[tok(d), _resident((1, d)), _resident(w_in.shape), _resident(pm.shape),
                  _resident((1, A_WIDTH)), _resident((1, A_WIDTH)),
                  tok(LANES), tok(LANES), tok(LANES),
                  _resident(dcos.shape), _resident(dsin.shape), _resident(mbd.shape)],
        out_specs=[tok(A_WIDTH), tok(A_WIDTH), tok(A_WIDTH), tok(B_WIDTH), tok(B_WIDTH)],
        out_shape=[jax.ShapeDtypeStruct((b, s, A_WIDTH), F32)] * 3
        + [jax.ShapeDtypeStruct((b, s, B_WIDTH), BF16)] * 2,
        scratch_shapes=[pltpu.VMEM((B_WIDTH, 2 * B_WIDTH), BF16)],
        compiler_params=_cparams(("arbitrary", "arbitrary")),
        name="even_in",
    )(x, g, w_in, pm, gq, gk, cos, s_up, s_dn, dcos, dsin, mbd)


DIL_ONES_ROWS = 16
SH_TQ = LANES
SH_TK = 2 * LANES
SH_QPAD = RADIUS
SH_KPAD = LANES
SH_PAIR = 2


def _dilated_sh_body(q_ref, k_ref, v_ref, o_ref, qd, kd, vt, o_st, l_st, out_scr, s_scr, p_scr,
                     bias_scr, *, seq):
    lane = lax.broadcasted_iota(jnp.int32, (1, LANES), 1)
    head_lanes = (lane < HEAD_DIM, lane >= HEAD_DIM)
    kk = lax.broadcasted_iota(jnp.int32, (SH_TK, SH_TQ), 0)
    qq = lax.broadcasted_iota(jnp.int32, (SH_TK, SH_TQ), 1)
    band = jnp.abs(kk - qq - (SH_KPAD - SH_QPAD)) <= RADIUS
    key_ok = (kk >= SH_KPAD, kk >= 0, kk < SH_KPAD)
    for idx in range(3):
        bias_scr[idx] = jnp.where(jnp.logical_and(band, key_ok[idx]), 0.0, NEG_INF)
    ones = jnp.ones((DIL_ONES_ROWS, SH_TK), BF16)
    max_dil = max(DILATIONS)
    state_pad = SH_QPAD * max_dil

    order = tuple(reversed(DILATIONS))
    for dil in order:
        first, last = dil == order[0], dil == order[-1]
        sub_len = seq // dil
        chunks = sub_len // LANES
        tiles_per_class = chunks + 1
        n_tiles = dil * tiles_per_class
        n_items = -(-n_tiles // SH_PAIR)
        assert last or n_tiles % SH_PAIR == 0
        q_stride = sub_len + 2 * SH_QPAD
        k_stride = sub_len + 2 * SH_KPAD

        qd[...] = jnp.zeros(qd.shape, BF16)
        kd[...] = jnp.zeros(kd.shape, BF16)
        vt[...] = jnp.zeros(vt.shape, BF16)

        def prepare(idx, carry, dil=dil, chunks=chunks, q_stride=q_stride, k_stride=k_stride):
            res = idx // chunks
            c0 = (idx - res * chunks) * LANES
            if dil == 1:
                src = pl.ds(pl.multiple_of(c0, LANES), LANES)
            else:
                src = pl.ds(res + dil * c0, LANES, stride=dil)
            k_dst = pl.ds(pl.multiple_of(res * k_stride + SH_KPAD + c0, LANES), LANES)
            qd[pl.ds(pl.multiple_of(res * q_stride + SH_QPAD + c0, SH_QPAD), LANES), :] = (
                q_ref[src, :].astype(BF16))
            kd[k_dst, :] = k_ref[src, :].astype(BF16)
            vt[:, k_dst] = v_ref[src, :].T.astype(BF16)
            return carry

        lax.fori_loop(0, seq // LANES, prepare, 0, unroll=4)

        def geometry(tile, tiles_per_class=tiles_per_class, n_tiles=n_tiles):
            tile = jnp.minimum(tile, n_tiles - 1)
            res = tile // tiles_per_class
            i = tile - res * tiles_per_class
            edge = (i > 0).astype(jnp.int32) + (i == tiles_per_class - 1).astype(jnp.int32)
            return res, i, edge

        def scores(item, slot, geometry=geometry, q_stride=q_stride, k_stride=k_stride):
            col_max = []
            for u in range(SH_PAIR):
                res, i, edge = geometry(item * SH_PAIR + u)
                k_win = kd[pl.ds(pl.multiple_of(res * k_stride + i * LANES, LANES), SH_TK), :]
                q2 = qd[pl.ds(pl.multiple_of(res * q_stride + i * LANES, LANES), SH_TQ), :]
                bias = bias_scr[edge]
                for h in range(2):
                    qh = jnp.where(head_lanes[h], q2, jnp.zeros_like(q2))
                    s = lax.dot_general(k_win, qh, (((1,), (1,)), ((), ())),
                                        preferred_element_type=F32) + bias
                    s_scr[slot, u, h] = s
                    col_max.append(jnp.max(s, axis=0, keepdims=True))
            return tuple(col_max)

        def exponentials(slot, col_max):
            for u in range(SH_PAIR):
                for h in range(2):
                    p_scr[slot, u, h] = jnp.exp2(s_scr[slot, u, h] - col_max[2 * u + h]).astype(BF16)

        def values(item, slot, col_max, geometry=geometry, dil=dil, k_stride=k_stride,
                   first=first, last=last):
            for u in range(SH_PAIR):
                res, i, _ = geometry(item * SH_PAIR + u)
                cols = pl.ds(pl.multiple_of(res * k_stride + i * LANES, LANES), SH_TK)
                o_t, l_t = [], []
                for h in range(2):
                    vth = jnp.concatenate([vt[h * HEAD_DIM:(h + 1) * HEAD_DIM, cols], ones], axis=0)
                    acc = _dot(vth, p_scr[slot, u, h])
                    denom = acc[HEAD_DIM:HEAD_DIM + 1]
                    o_t.append(acc[:HEAD_DIM] / denom)
                    l_t.append(jnp.broadcast_to(col_max[2 * u + h] + jnp.log2(denom),
                                                (HEAD_DIM, SH_TQ)))
                o_new = jnp.concatenate(o_t, axis=0).T
                l_new = jnp.concatenate(l_t, axis=0).T
                start = state_pad + res + dil * (i * LANES - SH_QPAD)
                if dil == 1:
                    st_rows = pl.ds(pl.multiple_of(start, SH_QPAD), SH_TQ)
                else:
                    st_rows = pl.ds(start, SH_TQ, stride=dil)
                if not first:
                    o_old, l_old = o_st[st_rows, :], l_st[st_rows, :]
                    l_max = jnp.maximum(l_old, l_new)
                    w_old, w_new = jnp.exp2(l_old - l_max), jnp.exp2(l_new - l_max)
                    w_sum = w_old + w_new
                    o_new = (w_old * o_old + w_new * o_new) / w_sum
                    l_new = l_max + jnp.log2(w_sum)
                if last:
                    out_scr[pl.ds(pl.multiple_of(i * LANES, LANES), SH_TQ), :] = o_new.astype(BF16)
                else:
                    o_st[st_rows, :] = o_new
                    l_st[st_rows, :] = l_new

        def tick(t, parity, carry, scores=scores, exponentials=exponentials, values=values):
            cm_prev, cm_prev2 = carry
            cm_new = scores(t, parity)
            exponentials(1 - parity, cm_prev)
            values(t - 2, parity, cm_prev2)
            return cm_new, cm_prev

        cm0 = scores(0, 0)
        cm1 = scores(1, 1)
        exponentials(0, cm0)

        def two_ticks(j, carry, tick=tick):
            t = 2 * j + 2
            return tick(t + 1, 1, tick(t, 0, carry))

        loops = (n_items - 2) // 2
        carry = lax.fori_loop(0, loops, two_ticks, (cm1, cm0))
        for t in range(2 + 2 * loops, n_items):
            carry = tick(t, t % 2, carry)
        cm_last, cm_prev = carry
        exponentials((n_items - 1) % 2, cm_last)
        values(n_items - 2, n_items % 2, cm_prev)
        values(n_items - 1, (n_items - 1) % 2, cm_last)

    o_ref[...] = out_scr[SH_QPAD:SH_QPAD + seq, :]


def _dilated_sh(q, k, v):
    b, s, w = q.shape
    max_dil = max(DILATIONS)
    assert DILATIONS[0] == 1 and s % (max_dil * LANES) == 0
    spec = pl.BlockSpec((None, s, LANES), lambda bi, p: (bi, 0, p))
    return pl.pallas_call(
        functools.partial(_dilated_sh_body, seq=s),
        grid=(b, w // LANES),
        in_specs=[pl.BlockSpec((None, s, LANES), lambda bi, p: (bi, 0, p),
                               pipeline_mode=pl.Buffered(1))] * 3,
        out_specs=spec,
        out_shape=jax.ShapeDtypeStruct((b, s, w), BF16),
        scratch_shapes=[pltpu.VMEM((s + 2 * SH_QPAD * max_dil, LANES), BF16),
                        pltpu.VMEM((s + 2 * SH_KPAD * max_dil, LANES), BF16),
                        pltpu.VMEM((LANES, s + 2 * SH_KPAD * max_dil), BF16),
                        pltpu.VMEM((s + 2 * SH_QPAD * max_dil, LANES), F32),
                        pltpu.VMEM((s + 2 * SH_QPAD * max_dil, LANES), F32),
                        pltpu.VMEM((s + 2 * SH_QPAD, LANES), BF16),
                        pltpu.VMEM((2, SH_PAIR, 2, SH_TK, SH_TQ), F32),
                        pltpu.VMEM((2, SH_PAIR, 2, SH_TK, SH_TQ), BF16),
                        pltpu.VMEM((3, SH_TK, SH_TQ), F32)],
        compiler_params=_cparams(("arbitrary", "arbitrary")),
        name="dilated",
    )(q, k, v)


FFT_N2 = 128


def _fft1_body(zr_ref, zi_ref, f_ref, twc_ref, tws_ref, yr_ref, yi_ref, *, n1):
    zz = jnp.concatenate([zr_ref[...], zi_ref[...]], axis=0)
    y = _dot(f_ref[...], zz)
    yr, yi = y[:n1], y[n1:]
    c, s = twc_ref[...], tws_ref[...]
    yr_ref[...] = (yr * c + yi * s).astype(BF16)
    yi_ref[...] = (yi * c - yr * s).astype(BF16)


def _fft1(zr, zi, fmat, twc, tws, *, wb):
    b, n1, w = zr.shape
    dat = pl.BlockSpec((None, n1, wb), lambda bi, j: (bi, 0, j))
    tw = pl.BlockSpec((n1, wb), lambda bi, j: (0, j))
    return pl.pallas_call(
        functools.partial(_fft1_body, n1=n1),
        grid=(b, w // wb),
        in_specs=[dat, dat, _resident(fmat.shape), tw, tw],
        out_specs=[dat, dat],
        out_shape=[jax.ShapeDtypeStruct((b, n1, w), BF16)] * 2,
        compiler_params=_cparams(("arbitrary", "arbitrary")),
        name="fft_stage1",
    )(zr, zi, fmat, twc, tws)


def _fft2_body(yr_ref, yi_ref, f_ref, o_ref, *, kb):
    f = f_ref[...]
    for j in range(kb):
        rs = pl.ds(j * FFT_N2, FFT_N2)
        yy = jnp.concatenate([yr_ref[rs, :], yi_ref[rs, :]], axis=0)
        o_ref[:, j * B_WIDTH:(j + 1) * B_WIDTH] = _dot(f, yy).astype(o_ref.dtype)


def _fft2(yr, yi, fmat, *, kb):
    b, rows, w = yr.shape
    n1 = rows // FFT_N2
    dat = pl.BlockSpec((None, kb * FFT_N2, w), lambda bi, j: (bi, j, 0))
    return pl.pallas_call(
        functools.partial(_fft2_body, kb=kb),
        grid=(b, n1 // kb),
        in_specs=[dat, dat, _resident(fmat.shape)],
        out_specs=pl.BlockSpec((None, FFT_N2, kb * w), lambda bi, j: (bi, 0, j)),
        out_shape=jax.ShapeDtypeStruct((b, FFT_N2, n1 * w), BF16),
        compiler_params=_cparams(("arbitrary", "arbitrary")),
        name="fft_stage2",
    )(yr, yi, fmat)


def _odd_in_body(x_ref, g_ref, w_ref, gql_ref, wq_ref, gkvl_ref, wkv_ref, gq_ref, gk_ref,
                 cos_ref, sin_ref, q_ref, k_ref, v_ref, u_ref):
    h = _rms_rows(x_ref[...], g_ref[...]).astype(BF16)
    z = _dot(h, w_ref[...])
    cos, sin = cos_ref[...], sin_ref[...]
    kv_off = C_Q_RANK
    pe_off = kv_off + C_KV_RANK
    rot_off = pe_off + LANES
    u_off = rot_off + LANES
    hw = N_HEADS_C * LANES
    q_lat = _rms_rows(z[:, :C_Q_RANK], gql_ref[...]).astype(BF16)
    kv_lat = _rms_rows(z[:, kv_off:pe_off], gkvl_ref[...]).astype(BF16)
    k_pe = z[:, pe_off:rot_off]
    k_pe_rot = z[:, rot_off:u_off]
    qf = _dot(q_lat, wq_ref[...])
    kvf = _dot(kv_lat, wkv_ref[...])

    def head(t, t_rot, gains, scale):
        ms = jnp.sum(t * t, axis=-1, keepdims=True) * (1.0 / C_QK_DIM)
        r = lax.rsqrt(ms + NORM_EPS) * scale
        return (t * (gains[0:1] * cos) + t_rot * (gains[1:2] * sin)) * r

    gq, gk = gq_ref[...], gk_ref[...]
    for hd in range(N_HEADS_C):
        sl = slice(hd * LANES, (hd + 1) * LANES)
        rot_sl = slice(hw + hd * LANES, hw + (hd + 1) * LANES)
        q_ref[:, sl] = head(qf[:, sl], qf[:, rot_sl], gq,
                            C_QK_DIM ** -0.5 * math.log2(math.e)).astype(BF16)
        k_ref[:, sl] = head(kvf[:, sl] + k_pe, k_pe_rot, gk, 1.0).astype(BF16)
    v_ref[...] = kvf[:, hw:].astype(BF16)
    u_ref[...] = z[:, u_off:]


def _odd_in(x, g, w_in, gql, wq, gkvl, wkv, gq, gk, cos, sin, *, ts):
    b, s, d = x.shape
    tok = lambda w: pl.BlockSpec((None, ts, w), lambda bi, i: (bi, i, 0))
    hw = N_HEADS_C * LANES
    return pl.pallas_call(
        _odd_in_body,
        grid=(b, s // ts),
        in_specs=[tok(d), _resident((1, d)), _resident(w_in.shape), _resident(gql.shape),
                  _resident(wq.shape), _resident(gkvl.shape), _resident(wkv.shape),
                  _resident(gq.shape), _resident(gk.shape), tok(LANES), tok(LANES)],
        out_specs=[tok(hw), tok(hw), tok(N_HEADS_C * C_V_DIM),
                   pl.BlockSpec((ts, D_WIDTH), lambda bi, i: (i, bi))],
        out_shape=[jax.ShapeDtypeStruct((b, s, hw), BF16), jax.ShapeDtypeStruct((b, s, hw), BF16),
                   jax.ShapeDtypeStruct((b, s, N_HEADS_C * C_V_DIM), BF16),
                   jax.ShapeDtypeStruct((s, b * D_WIDTH), F32)],
        compiler_params=_cparams(("arbitrary", "arbitrary")),
        name="odd_in",
    )(x, g, w_in, gql, wq, gkvl, wkv, gq, gk, cos, sin)


FLASH_ONES_ROWS = 16


def _flash_body(q_ref, k_ref, vt_ref, o_ref, s_scr, p_scr, *, tq, tk):
    n_q = q_ref.shape[0] // tq
    n_kv = k_ref.shape[0] // tk
    heads = ((0, 0), (LANES, C_V_DIM))
    ones = jnp.ones((FLASH_ONES_ROWS, tk), BF16)
    items = [(qi, kj) for qi in range(n_q) for kj in range(n_kv)]

    def scores(item, slot):
        qi, kj = item
        col_max = []
        for h, (off, _) in enumerate(heads):
            s = lax.dot_general(k_ref[kj * tk:(kj + 1) * tk, off:off + LANES],
                                q_ref[qi * tq:(qi + 1) * tq, off:off + LANES],
                                (((1,), (1,)), ((), ())), preferred_element_type=F32)
            s_scr[slot, h] = s
            col_max.append(jnp.max(s, axis=0, keepdims=True))
        return col_max

    def softmax(slot, m2, col_max2):
        m_out, alpha_out = [], []
        for h in range(2):
            m_new = jnp.maximum(m2[h], col_max2[h])
            p_scr[slot, h] = jnp.exp2(s_scr[slot, h] - m_new).astype(BF16)
            alpha_out.append(jnp.exp2(m2[h] - m_new))
            m_out.append(m_new)
        return m_out, alpha_out

    def accumulate(item, slot, acc2, alpha2):
        _, kj = item
        out = []
        for h, (_, voff) in enumerate(heads):
            vt = jnp.concatenate([vt_ref[voff:voff + C_V_DIM, kj * tk:(kj + 1) * tk], ones], axis=0)
            out.append(acc2[h] * alpha2[h] + _dot(vt, p_scr[slot, h]))
        return out

    m0 = jnp.full((1, tq), NEG_INF, F32)
    a0 = jnp.zeros((C_V_DIM + FLASH_ONES_ROWS, tq), F32)
    m_state = [[m0, m0] for _ in range(n_q)]
    acc_state = [[a0, a0] for _ in range(n_q)]
    col, alpha = {}, {}
    for t in range(len(items) + 2):
        if t < len(items):
            col[t] = scores(items[t], t % 2)
        if 0 <= t - 1 < len(items):
            qi = items[t - 1][0]
            m_state[qi], alpha[t - 1] = softmax((t - 1) % 2, m_state[qi], col.pop(t - 1))
        if 0 <= t - 2 < len(items):
            qi, kj = items[t - 2]
            acc_state[qi] = accumulate(items[t - 2], t % 2, acc_state[qi], alpha.pop(t - 2))
            if kj == n_kv - 1:
                out_t = jnp.concatenate(
                    [acc[:C_V_DIM] / acc[C_V_DIM:C_V_DIM + 1] for acc in acc_state[qi]], axis=0)
                o_ref[qi * tq:(qi + 1) * tq, :] = out_t.T.astype(o_ref.dtype)


def _flash(q, k, vt, *, tq, tk, n_q):
    b, s, hw = q.shape
    pairs = hw // (2 * LANES)
    bq = tq * n_q
    assert s % bq == 0 and s % tk == 0
    return pl.pallas_call(
        functools.partial(_flash_body, tq=tq, tk=tk),
        grid=(b, pairs, s // bq),
        in_specs=[pl.BlockSpec((None, bq, 2 * LANES), lambda bi, p, i: (bi, i, p)),
                  pl.BlockSpec((None, s, 2 * LANES), lambda bi, p, i: (bi, 0, p)),
                  pl.BlockSpec((None, 2 * C_V_DIM, s), lambda bi, p, i: (bi, p, 0))],
        out_specs=pl.BlockSpec((None, bq, LANES), lambda bi, p, i: (bi, i, p)),
        out_shape=jax.ShapeDtypeStruct((b, s, pairs * LANES), BF16),
        scratch_shapes=[pltpu.VMEM((2, 2, tk, tq), F32), pltpu.VMEM((2, 2, tk, tq), BF16)],
        compiler_params=_cparams(("arbitrary", "arbitrary", "arbitrary")),
        name="flash",
    )(q, k, vt)


def _s5_body(u_ref, bm_ref, cm_ref, lam_ref, y_ref, bu_scr, xs_scr, st_scr, *, tc, nb):
    direction = pl.program_id(0)

    @pl.when(pl.program_id(1) == 0)
    def _():
        st_scr[...] = jnp.zeros(st_scr.shape, F32)

    bu_scr[...] = _dot(u_ref[...].astype(BF16), bm_ref[...])
    a_re = jnp.broadcast_to(lam_ref[0:1, :], (nb, S5_LANES))
    a_im = jnp.broadcast_to(lam_ref[1:2, :], (nb, S5_LANES))

    def step(t, carry):
        x_re, x_im = carry
        tt = jnp.where(direction == 0, t, tc - 1 - t)
        rs = pl.ds(pl.multiple_of(tt * nb, nb), nb)
        n_re = a_re * x_re - a_im * x_im + bu_scr[rs, :S5_LANES]
        n_im = a_re * x_im + a_im * x_re + bu_scr[rs, S5_LANES:]
        xs_scr[rs, :S5_LANES] = n_re
        xs_scr[rs, S5_LANES:] = n_im
        return n_re, n_im

    x_re, x_im = lax.fori_loop(0, tc, step, (st_scr[:, :S5_LANES], st_scr[:, S5_LANES:]),
                               unroll=4)
    st_scr[:, :S5_LANES] = x_re
    st_scr[:, S5_LANES:] = x_im
    y_ref[...] = _dot(xs_scr[...].astype(BF16), cm_ref[...])


def _s5_scan(u2, bmat, cmat, lam, *, tc, nb):
    rows, dw = u2.shape
    n_chunks = rows // (tc * nb)

    def chunk(d, c):
        return jnp.where(d == 0, c, n_chunks - 1 - c)

    return pl.pallas_call(
        functools.partial(_s5_body, tc=tc, nb=nb),
        grid=(2, n_chunks),
        in_specs=[pl.BlockSpec((tc * nb, dw), lambda d, c: (chunk(d, c), 0)),
                  pl.BlockSpec((None, dw, 2 * S5_LANES), lambda d, c: (d, 0, 0)),
                  pl.BlockSpec((None, 2 * S5_LANES, dw), lambda d, c: (d, 0, 0)),
                  pl.BlockSpec((None, 2, S5_LANES), lambda d, c: (d, 0, 0))],
        out_specs=pl.BlockSpec((None, tc * nb, dw), lambda d, c: (d, chunk(d, c), 0)),
        out_shape=jax.ShapeDtypeStruct((2, rows, dw), F32),
        scratch_shapes=[pltpu.VMEM((tc * nb, 2 * S5_LANES), F32),
                        pltpu.VMEM((tc * nb, 2 * S5_LANES), F32),
                        pltpu.VMEM((nb, 2 * S5_LANES), F32)],
        compiler_params=_cparams(("arbitrary", "arbitrary")),
        name="s5_scan",
    )(u2, bmat, cmat, lam)


def _s5_glu_body(y_ref, u_ref, dskip_ref, w_ref, b_ref, o_ref):
    y = y_ref[0] + y_ref[1] + u_ref[...] * dskip_ref[...]
    z = jax.nn.gelu(y)
    gate = jax.nn.sigmoid(_dot(z.astype(BF16), w_ref[...]) + b_ref[...])
    o_ref[...] = (z * gate).astype(o_ref.dtype)


def _s5_glu(y, u2, dskip, w, bias, *, tr):
    rows, dw = u2.shape
    return pl.pallas_call(
        _s5_glu_body,
        grid=(rows // tr,),
        in_specs=[pl.BlockSpec((2, tr, dw), lambda i: (0, i, 0)),
                  pl.BlockSpec((tr, dw), lambda i: (i, 0)),
                  _resident((1, dw)), _resident(w.shape), _resident((1, dw))],
        out_specs=pl.BlockSpec((tr, dw), lambda i: (i, 0)),
        out_shape=jax.ShapeDtypeStruct((rows, dw), BF16),
        compiler_params=_cparams(("arbitrary",)),
        name="s5_glu",
    )(y, u2, dskip, w, bias)


def _rotary_tables(positions, rot_dim, lane_of_first, period):
    half = rot_dim // 2
    inv_freq = ROPE_THETA ** (-jnp.arange(0, rot_dim, 2, dtype=F32) / rot_dim)
    ang = positions.astype(F32)[..., None] * inv_freq
    ang = jnp.concatenate([ang, ang], axis=-1)
    cos_r, sin_r = jnp.cos(ang), jnp.sin(ang)
    lane = np.arange(LANES)
    j = (lane - lane_of_first) % period
    in_rot = (lane >= lane_of_first) & (j < rot_dim)
    idx = np.where(in_rot, j, 0)
    cos = jnp.where(in_rot, cos_r[..., idx], 1.0)
    sin = jnp.where(in_rot, sin_r[..., idx], 0.0)
    s_up = jnp.where(in_rot & (j < half), -sin, 0.0)
    s_dn = jnp.where(in_rot & (j >= half), sin, 0.0)
    return cos, s_up, s_dn


def _dft_mats(n):
    ang = 2.0 * np.pi * np.outer(np.arange(n), np.arange(n)) / n
    return np.cos(ang), np.sin(ang)


def _block_diag(blocks):
    g, r, c = blocks.shape
    eye = jnp.eye(g, dtype=blocks.dtype)
    return jnp.einsum("grc,gh->grhc", blocks, eye).reshape(g * r, g * c)


def _pick_tile(n, pref):
    t = min(n, pref)
    while n % t:
        t //= 2
    return t


def kernel(x, positions, ffn1_norm, ffn1_w_gate, ffn1_w_up, ffn1_w_down, mix_norm, ffn2_norm, ffn2_w_gate, ffn2_w_up, ffn2_w_down, ab_w_in, ab_w_out, a_q_norm, a_k_norm, b_w_mix, cd_w_in, cd_w_out, c_q_lat_norm, c_w_q_up, c_kv_lat_norm, c_w_kv_up, c_q_norm, c_k_norm, d_lam_re, d_lam_im, d_log_step, d_b_re, d_b_im, d_c_re, d_c_im, d_skip, d_w_glu, d_b_glu):
    bsz, seq, d_model = x.shape
    depth = ffn1_norm.shape[0]
    tokens = bsz * seq
    ts = _pick_tile(seq, 512)
    assert bsz % SUBLANES == 0
    tm = _pick_tile(seq, 512)
    n1 = seq // FFT_N2
    row = lambda v: v.reshape(1, -1).astype(F32)

    cos_a, sup_a, sdn_a = _rotary_tables(positions, ROT_DIM_A, 0, HEAD_DIM)
    cos_c, sup_c, sdn_c = _rotary_tables(positions, C_ROPE_DIM, C_NOPE_DIM, LANES)

    cc, sc_ = _dft_mats(FNET_GROUP_DIM)
    norm = 1.0 / math.sqrt(seq * FNET_GROUP_DIM)
    eye_g = np.eye(FNET_GROUPS)
    dcos = jnp.asarray(np.kron(eye_g, cc) * norm, BF16)
    dsin = jnp.asarray(np.kron(eye_g, -sc_) * norm, BF16)
    c1, s1 = _dft_mats(n1)
    f1 = jnp.asarray(np.block([[c1, s1], [-s1, c1]]), BF16)
    c2, s2 = _dft_mats(FFT_N2)
    f2 = jnp.asarray(np.concatenate([c2, s2], axis=1), BF16)
    tw_ang = 2.0 * np.pi * np.outer(np.arange(n1), np.arange(FFT_N2)) / seq
    twc = jnp.repeat(jnp.asarray(np.cos(tw_ang), F32), B_WIDTH, axis=1)
    tws = jnp.repeat(jnp.asarray(np.sin(tw_ang), F32), B_WIDTH, axis=1)
    head_mean = jnp.asarray(np.kron(np.eye(N_HEADS_A), np.full((HEAD_DIM, HEAD_DIM), 1.0 / HEAD_DIM)), BF16)

    x2 = x.reshape(tokens, d_model)
    for layer in range(depth):
        i = layer // 2
        x2 = _ffn(x2, row(ffn1_norm[layer]), ffn1_w_gate[layer].astype(BF16),
                  ffn1_w_up[layer].astype(BF16), ffn1_w_down[layer].astype(BF16), tm=tm)
        x3 = x2.reshape(bsz, seq, d_model)
        if layer % 2 == 0:
            mbd = _block_diag(b_w_mix[i]).astype(BF16)
            q, k, v, zr, zi = _even_in(
                x3, row(mix_norm[layer]), ab_w_in[i].astype(BF16), head_mean,
                row(jnp.tile(a_q_norm[i], N_HEADS_A)), row(jnp.tile(a_k_norm[i], N_HEADS_A)),
                cos_a, sup_a, sdn_a, dcos, dsin, mbd, ts=ts)
            a_out = _dilated_sh(q, k, v)
            shp = (bsz, n1, FFT_N2 * B_WIDTH)
            yr, yi = _fft1(zr.reshape(shp), zi.reshape(shp), f1, twc, tws,
                           wb=_pick_tile(FFT_N2 * B_WIDTH, 4096))
            shp2 = (bsz, n1 * FFT_N2, B_WIDTH)
            spec = _fft2(yr.reshape(shp2), yi.reshape(shp2), f2, kb=_pick_tile(n1, 8))
            w_out = ab_w_out[i].astype(BF16)
            mixer = (a_out.reshape(tokens, A_WIDTH), spec.reshape(tokens, B_WIDTH),
                     w_out[:A_WIDTH], w_out[A_WIDTH:],
                     pl.BlockSpec((tm, B_WIDTH), lambda r: (r, 0)))
        else:
            half = C_ROPE_DIM // 2
            rot_cols = lambda w: jnp.concatenate([-w[..., half:], w[..., :half]], axis=-1)
            swap_halves = lambda w: jnp.concatenate([w[..., half:], w[..., :half]], axis=-1)
            rope_slab = lambda w: jnp.pad(
                w, [(0, 0)] * (w.ndim - 1) + [(C_NOPE_DIM, LANES - C_QK_DIM)])
            w_in = cd_w_in[i]
            pe_off = C_Q_RANK + C_KV_RANK
            w_pe = w_in[:, pe_off:pe_off + C_ROPE_DIM]
            w_in_p = jnp.concatenate([w_in[:, :pe_off], rope_slab(w_pe), rope_slab(rot_cols(w_pe)),
                                      w_in[:, pe_off + C_ROPE_DIM:]], axis=1).astype(BF16)
            pad_h = lambda w, keep: jnp.pad(
                w, ((0, 0), (0, 0), (0, LANES - keep))).reshape(w.shape[0], N_HEADS_C * LANES)
            wq = c_w_q_up[i].reshape(C_Q_RANK, N_HEADS_C, C_QK_DIM)
            wq_p = jnp.concatenate(
                [pad_h(wq, C_QK_DIM),
                 rope_slab(rot_cols(wq[..., C_NOPE_DIM:])).reshape(C_Q_RANK, N_HEADS_C * LANES)],
                axis=1).astype(BF16)
            wkv = c_w_kv_up[i].reshape(C_KV_RANK, N_HEADS_C, C_NOPE_DIM + C_V_DIM)
            wkv_p = jnp.concatenate(
                [pad_h(wkv[:, :, :C_NOPE_DIM], C_NOPE_DIM),
                 wkv[:, :, C_NOPE_DIM:].reshape(C_KV_RANK, N_HEADS_C * C_V_DIM)], axis=1).astype(BF16)
            gains = lambda gvec: jnp.stack(
                [jnp.pad(gvec, (0, LANES - C_QK_DIM)),
                 rope_slab(swap_halves(gvec[C_NOPE_DIM:]))]).astype(F32)
            q, k, v, u_t = _odd_in(
                x3, row(mix_norm[layer]), w_in_p, row(c_q_lat_norm[i]), wq_p,
                row(c_kv_lat_norm[i]), wkv_p, gains(c_q_norm[i]), gains(c_k_norm[i]),
                cos_c, sdn_c - sup_c, ts=ts)
            c_out = _flash(q, k, jnp.swapaxes(v, 1, 2), tq=_pick_tile(seq, 256),
                           tk=_pick_tile(seq, 512), n_q=2)

            lam = lax.complex(d_lam_re[i].astype(F32), d_lam_im[i].astype(F32))
            step = jnp.exp(d_log_step[i].astype(F32))[..., None]
            lam_bar = jnp.exp(lam * step)
            b_bar = ((lam_bar - 1.0) / lam)[..., None] * lax.complex(
                d_b_re[i].astype(F32), d_b_im[i].astype(F32))
            to_bd = lambda m: jnp.stack([_block_diag(m[d]) for d in range(2)])
            b_t = jnp.swapaxes(b_bar, -1, -2)
            bmat = jnp.concatenate([to_bd(jnp.real(b_t)), to_bd(jnp.imag(b_t))], axis=-1).astype(BF16)
            c_re_t = jnp.swapaxes(d_c_re[i].astype(F32), -1, -2)
            c_im_t = jnp.swapaxes(d_c_im[i].astype(F32), -1, -2)
            cmat = jnp.concatenate([to_bd(c_re_t), to_bd(-c_im_t)], axis=-2).astype(BF16)
            lam2 = jnp.stack([jnp.real(lam_bar).reshape(2, S5_LANES),
                              jnp.imag(lam_bar).reshape(2, S5_LANES)], axis=1)

            u2 = u_t.reshape(seq * bsz, D_WIDTH)
            tc = _pick_tile(seq, 128)
            y = _s5_scan(u2, bmat, cmat, lam2, tc=tc, nb=bsz)
            d2 = _s5_glu(y, u2, row(d_skip[i]), d_w_glu[i].astype(BF16), row(d_b_glu[i]),
                         tr=_pick_tile(seq * bsz, 1024))
            d_out = d2.reshape(seq, bsz * D_WIDTH)
            w_out = cd_w_out[i].astype(BF16)
            tiles_per_seq = seq // tm
            mixer = (c_out.reshape(tokens, N_HEADS_C * C_V_DIM), d_out,
                     w_out[:N_HEADS_C * C_V_DIM], w_out[N_HEADS_C * C_V_DIM:],
                     pl.BlockSpec((tm, D_WIDTH),
                                  lambda r: (r % tiles_per_seq, r // tiles_per_seq)))
        x2 = _ffn(x2, row(ffn2_norm[layer]), ffn2_w_gate[layer].astype(BF16),
                  ffn2_w_up[layer].astype(BF16), ffn2_w_down[layer].astype(BF16), tm=tm,
                  mixer=mixer)
    return x2.reshape(bsz, seq, d_model)
```

```python
import functools
import math

import jax
import jax.numpy as jnp
import numpy as np
from jax import lax
from jax.experimental import pallas as pl
from jax.experimental.pallas import tpu as pltpu

F32 = jnp.float32
BF16 = jnp.bfloat16

HEAD_DIM = 64
N_HEADS_A = 8
ROT_DIM_A = 16
DILATIONS = (1, 4, 16)
RADIUS = 64
FNET_GROUPS = 4
FNET_GROUP_DIM = 64
N_HEADS_C = 8
C_NOPE_DIM = 64
C_ROPE_DIM = 32
C_V_DIM = 64
C_QK_DIM = C_NOPE_DIM + C_ROPE_DIM
C_Q_RANK = 256
C_KV_RANK = 128
S5_GROUPS = 16
S5_GROUP_DIM = 16
S5_STATE = 64
ROPE_THETA = 500000.0
NORM_EPS = 1e-6
NEG_INF = -1e30
A_WIDTH = N_HEADS_A * HEAD_DIM
B_WIDTH = FNET_GROUPS * FNET_GROUP_DIM
D_WIDTH = S5_GROUPS * S5_GROUP_DIM
S5_LANES = S5_GROUPS * S5_STATE

LANES = 128
SUBLANES = 8
VMEM_LIMIT_BYTES = 56 * 1024 * 1024


def _cparams(sem):
    return pltpu.CompilerParams(dimension_semantics=sem, vmem_limit_bytes=VMEM_LIMIT_BYTES)


def _resident(shape):
    nd = len(shape)
    return pl.BlockSpec(shape, lambda *_: (0,) * nd, pipeline_mode=pl.Buffered(1))


def _rms_rows(x, g):
    ms = jnp.mean(x * x, axis=-1, keepdims=True)
    return x * lax.rsqrt(ms + NORM_EPS) * g


def _dot(a, b):
    return jnp.dot(a, b, preferred_element_type=F32)


FFN_CHUNK = 256


def _ffn_body(*refs, n_chunks, fused_mixer):
    if fused_mixer:
        x_ref, a_ref, b_ref, wa_ref, wb_ref, g_ref, wg_ref, wu_ref, wd_ref, o_ref = refs
        x = x_ref[...] + _dot(a_ref[...], wa_ref[...]) + _dot(b_ref[...], wb_ref[...])
    else:
        x_ref, g_ref, wg_ref, wu_ref, wd_ref, o_ref = refs
        x = x_ref[...]
    h = _rms_rows(x, g_ref[...]).astype(BF16)
    acc = jnp.zeros(x.shape, F32)
    for c in range(n_chunks):
        sl = pl.ds(c * FFN_CHUNK, FFN_CHUNK)
        gate = _dot(h, wg_ref[:, sl])
        up = _dot(h, wu_ref[:, sl])
        act = (gate * jax.nn.sigmoid(gate) * up).astype(BF16)
        acc = acc + _dot(act, wd_ref[sl, :])
    o_ref[...] = x + 0.5 * acc


def _ffn(x2, g, wg, wu, wd, *, tm, mixer=None):
    t, d = x2.shape
    f = wg.shape[1]
    assert t % tm == 0 and f % FFN_CHUNK == 0
    rows = lambda w: pl.BlockSpec((tm, w), lambda i: (i, 0))
    weights = [_resident((1, d)), _resident((d, f)), _resident((d, f)), _resident((f, d))]
    if mixer is None:
        in_specs, args = [rows(d)] + weights, (x2, g, wg, wu, wd)
    else:
        a, b, wa, wb, b_spec = mixer
        in_specs = [rows(d), rows(a.shape[1]), b_spec, _resident(wa.shape), _resident(wb.shape)] + weights
        args = (x2, a, b, wa, wb, g, wg, wu, wd)
    return pl.pallas_call(
        functools.partial(_ffn_body, n_chunks=f // FFN_CHUNK, fused_mixer=mixer is not None),
        grid=(t // tm,),
        in_specs=in_specs,
        out_specs=rows(d),
        out_shape=jax.ShapeDtypeStruct((t, d), F32),
        compiler_params=_cparams(("arbitrary",)),
        name="ffn_mix" if mixer is not None else "ffn",
    )(*args)


def _rotary_block(t, cos, s_up, s_dn, shift):
    return (t * cos + pltpu.roll(t, LANES - shift, axis=1) * s_up
            + pltpu.roll(t, shift, axis=1) * s_dn)


def _even_in_body(x_ref, g_ref, w_ref, pm_ref, gq_ref, gk_ref, cos_ref, sup_ref, sdn_ref,
                  dcos_ref, dsin_ref, mbd_ref,
                  q_ref, k_ref, v_ref, zr_ref, zi_ref, wc_scr):
    first = jnp.logical_and(pl.program_id(0) == 0, pl.program_id(1) == 0)

    @pl.when(first)
    def _():
        mbd = mbd_ref[...]
        wc_scr[:, :B_WIDTH] = _dot(dcos_ref[...], mbd).astype(BF16)
        wc_scr[:, B_WIDTH:] = _dot(dsin_ref[...], mbd).astype(BF16)

    h = _rms_rows(x_ref[...], g_ref[...]).astype(BF16)
    z = _dot(h, w_ref[...])
    cos, s_up, s_dn = cos_ref[...], sup_ref[...], sdn_ref[...]
    pm = pm_ref[...]

    def head_norm_rot(t, gain, scale):
        ms = _dot((t * t).astype(BF16), pm)
        tn = t * lax.rsqrt(ms + NORM_EPS) * gain
        blocks = [_rotary_block(tn[:, c * LANES:(c + 1) * LANES], cos, s_up, s_dn, ROT_DIM_A // 2)
                  for c in range(A_WIDTH // LANES)]
        out = jnp.concatenate(blocks, axis=1)
        return out * scale if scale != 1.0 else out

    q_ref[...] = head_norm_rot(z[:, :A_WIDTH], gq_ref[...], HEAD_DIM ** -0.5 * math.log2(math.e))
    k_ref[...] = head_norm_rot(z[:, A_WIDTH:2 * A_WIDTH], gk_ref[...], 1.0)
    v_ref[...] = z[:, 2 * A_WIDTH:3 * A_WIDTH]
    zz = _dot(z[:, 3 * A_WIDTH:].astype(BF16), wc_scr[...])
    zr_ref[...] = zz[:, :B_WIDTH].astype(BF16)
    zi_ref[...] = zz[:, B_WIDTH:].astype(BF16)


def _even_in(x, g, w_in, pm, gq, gk, cos, s_up, s_dn, dcos, dsin, mbd, *, ts):
    b, s, d = x.shape
    tok = lambda w: pl.BlockSpec((None, ts, w), lambda bi, i: (bi, i, 0))
    return pl.pallas_call(
        _even_in_body,
        grid=(b, s // ts),
        in_specs=[tok(d), _resident((1, d)), _resident(w_in.shape), _resident(pm.shape),
                  _resident((1, A_WIDTH)), _resident((1, A_WIDTH)),
                  tok(LANES), tok(LANES), tok(LANES),
                  _resident(dcos.shape), _resident(dsin.shape), _resident(mbd.shape)],
        out_specs=[tok(A_WIDTH), tok(A_WIDTH), tok(A_WIDTH), tok(B_WIDTH), tok(B_WIDTH)],
        out_shape=[jax.ShapeDtypeStruct((b, s, A_WIDTH), F32)] * 3
        + [jax.ShapeDtypeStruct((b, s, B_WIDTH), BF16)] * 2,
        scratch_shapes=[pltpu.VMEM((B_WIDTH, 2 * B_WIDTH), BF16)],
        compiler_params=_cparams(("arbitrary", "arbitrary")),
        name="even_in",
    )(x, g, w_in, pm, gq, gk, cos, s_up, s_dn, dcos, dsin, mbd)


DIL_ONES_ROWS = 16
SH_TQ = LANES
SH_TK = 2 * LANES
SH_QPAD = RADIUS
SH_KPAD = LANES
SH_PAIR = 4


def _dilated_sh_body(q_ref, k_ref, v_ref, o_ref, qd, kd, vt, o_st, l_st, out_scr, s_scr, p_scr,
                     bias_scr, *, seq):
    lane = lax.broadcasted_iota(jnp.int32, (1, LANES), 1)
    head_lanes = (lane < HEAD_DIM, lane >= HEAD_DIM)
    kk = lax.broadcasted_iota(jnp.int32, (SH_TK, SH_TQ), 0)
    qq = lax.broadcasted_iota(jnp.int32, (SH_TK, SH_TQ), 1)
    band = jnp.abs(kk - qq - (SH_KPAD - SH_QPAD)) <= RADIUS
    key_ok = (kk >= SH_KPAD, kk >= 0, kk < SH_KPAD)
    for idx in range(3):
        bias_scr[idx] = jnp.where(jnp.logical_and(band, key_ok[idx]), 0.0, NEG_INF)
    ones = jnp.ones((DIL_ONES_ROWS, SH_TK), BF16)
    max_dil = max(DILATIONS)
    state_pad = SH_QPAD * max_dil

    order = tuple(reversed(DILATIONS))
    for dil in order:
        first, last = dil == order[0], dil == order[-1]
        sub_len = seq // dil
        chunks = sub_len // LANES
        tiles_per_class = chunks + 1
        n_tiles = dil * tiles_per_class
        n_items = -(-n_tiles // SH_PAIR)
        assert last or n_tiles % SH_PAIR == 0
        q_stride = sub_len + 2 * SH_QPAD
        k_stride = sub_len + 2 * SH_KPAD

        def zero_pads(res, carry, sub_len=sub_len, q_stride=q_stride, k_stride=k_stride):
            zq = jnp.zeros((SH_QPAD, LANES), BF16)
            zk = jnp.zeros((SH_KPAD, LANES), BF16)
            for off in (0, SH_QPAD + sub_len):
                qd[pl.ds(pl.multiple_of(res * q_stride + off, SH_QPAD), SH_QPAD), :] = zq
            for off in (0, SH_KPAD + sub_len):
                pad = pl.ds(pl.multiple_of(res * k_stride + off, SH_KPAD), SH_KPAD)
                kd[pad, :] = zk
                vt[:, pad] = jnp.zeros((LANES, SH_KPAD), BF16)
            return carry

        lax.fori_loop(0, dil, zero_pads, 0)

        def prepare(idx, carry, dil=dil, chunks=chunks, q_stride=q_stride, k_stride=k_stride):
            res = idx // chunks
            c0 = (idx - res * chunks) * LANES
            if dil == 1:
                src = pl.ds(pl.multiple_of(c0, LANES), LANES)
            else:
                src = pl.ds(res + dil * c0, LANES, stride=dil)
            k_dst = pl.ds(pl.multiple_of(res * k_stride + SH_KPAD + c0, LANES), LANES)
            qd[pl.ds(pl.multiple_of(res * q_stride + SH_QPAD + c0, SH_QPAD), LANES), :] = (
                q_ref[src, :].astype(BF16))
            kd[k_dst, :] = k_ref[src, :].astype(BF16)
            vt[:, k_dst] = v_ref[src, :].T.astype(BF16)
            return carry

        lax.fori_loop(0, seq // LANES, prepare, 0, unroll=4)

        def geometry(tile, tiles_per_class=tiles_per_class, n_tiles=n_tiles):
            tile = jnp.minimum(tile, n_tiles - 1)
            res = tile // tiles_per_class
            i = tile - res * tiles_per_class
            edge = (i > 0).astype(jnp.int32) + (i == tiles_per_class - 1).astype(jnp.int32)
            return res, i, edge

        def scores(item, slot, geometry=geometry, q_stride=q_stride, k_stride=k_stride):
            col_max = []
            for u in range(SH_PAIR):
                res, i, edge = geometry(item * SH_PAIR + u)
                k_win = kd[pl.ds(pl.multiple_of(res * k_stride + i * LANES, LANES), SH_TK), :]
                q2 = qd[pl.ds(pl.multiple_of(res * q_stride + i * LANES, LANES), SH_TQ), :]
                bias = bias_scr[edge]
                for h in range(2):
                    qh = jnp.where(head_lanes[h], q2, jnp.zeros_like(q2))
                    s = lax.dot_general(k_win, qh, (((1,), (1,)), ((), ())),
                                        preferred_element_type=F32) + bias
                    s_scr[slot, u, h] = s
                    col_max.append(jnp.max(s, axis=0, keepdims=True))
            return tuple(col_max)

        def exponentials(slot, col_max):
            for u in range(SH_PAIR):
                for h in range(2):
                    p_scr[slot, u, h] = jnp.exp2(s_scr[slot, u, h] - col_max[2 * u + h]).astype(BF16)

        def values(item, slot, col_max, geometry=geometry, dil=dil, k_stride=k_stride,
                   first=first, last=last):
            for u in range(SH_PAIR):
                res, i, _ = geometry(item * SH_PAIR + u)
                cols = pl.ds(pl.multiple_of(res * k_stride + i * LANES, LANES), SH_TK)
                o_t, l_t = [], []
                for h in range(2):
                    vth = jnp.concatenate([vt[h * HEAD_DIM:(h + 1) * HEAD_DIM, cols], ones], axis=0)
                    acc = _dot(vth, p_scr[slot, u, h])
                    denom = acc[HEAD_DIM:HEAD_DIM + 1]
                    o_t.append(acc[:HEAD_DIM] / denom)
                    l_t.append(jnp.broadcast_to(col_max[2 * u + h] + jnp.log2(denom),
                                                (HEAD_DIM, SH_TQ)))
                o_new = jnp.concatenate(o_t, axis=0).T
                l_new = jnp.concatenate(l_t, axis=0).T
                start = state_pad + res + dil * (i * LANES - SH_QPAD)
                if dil == 1:
                    st_rows = pl.ds(pl.multiple_of(start, SH_QPAD), SH_TQ)
                else:
                    st_rows = pl.ds(start, SH_TQ, stride=dil)
                if not first:
                    o_old, l_old = o_st[st_rows, :], l_st[st_rows, :]
                    l_max = jnp.maximum(l_old, l_new)
                    w_old, w_new = jnp.exp2(l_old - l_max), jnp.exp2(l_new - l_max)
                    w_sum = w_old + w_new
                    o_new = (w_old * o_old + w_new * o_new) / w_sum
                    l_new = l_max + jnp.log2(w_sum)
                if last:
                    out_scr[pl.ds(pl.multiple_of(i * LANES, LANES), SH_TQ), :] = o_new.astype(BF16)
                else:
                    o_st[st_rows, :] = o_new
                    l_st[st_rows, :] = l_new

        def tick(t, parity, carry, scores=scores, exponentials=exponentials, values=values):
            cm_prev, cm_prev2 = carry
            cm_new = scores(t, parity)
            exponentials(1 - parity, cm_prev)
            values(t - 2, parity, cm_prev2)
            return cm_new, cm_prev

        cm0 = scores(0, 0)
        cm1 = scores(1, 1)
        exponentials(0, cm0)

        def two_ticks(j, carry, tick=tick):
            t = 2 * j + 2
            return tick(t + 1, 1, tick(t, 0, carry))

        loops = (n_items - 2) // 2
        carry = lax.fori_loop(0, loops, two_ticks, (cm1, cm0))
        for t in range(2 + 2 * loops, n_items):
            carry = tick(t, t % 2, carry)
        cm_last, cm_prev = carry
        exponentials((n_items - 1) % 2, cm_last)
        values(n_items - 2, n_items % 2, cm_prev)
        values(n_items - 1, (n_items - 1) % 2, cm_last)

    o_ref[...] = out_scr[SH_QPAD:SH_QPAD + seq, :]


def _dilated_sh(q, k, v):
    b, s, w = q.shape
    max_dil = max(DILATIONS)
    assert DILATIONS[0] == 1 and s % (max_dil * LANES) == 0
    spec = pl.BlockSpec((None, s, LANES), lambda bi, p: (bi, 0, p))
    return pl.pallas_call(
        functools.partial(_dilated_sh_body, seq=s),
        grid=(b, w // LANES),
        in_specs=[pl.BlockSpec((None, s, LANES), lambda bi, p: (bi, 0, p),
                               pipeline_mode=pl.Buffered(1))] * 3,
        out_specs=spec,
        out_shape=jax.ShapeDtypeStruct((b, s, w), BF16),
        scratch_shapes=[pltpu.VMEM((s + 2 * SH_QPAD * max_dil, LANES), BF16),
                        pltpu.VMEM((s + 2 * SH_KPAD * max_dil, LANES), BF16),
                        pltpu.VMEM((LANES, s + 2 * SH_KPAD * max_dil), BF16),
                        pltpu.VMEM((s + 2 * SH_QPAD * max_dil, LANES), F32),
                        pltpu.VMEM((s + 2 * SH_QPAD * max_dil, LANES), F32),
                        pltpu.VMEM((s + 2 * SH_QPAD, LANES), BF16),
                        pltpu.VMEM((2, SH_PAIR, 2, SH_TK, SH_TQ), F32),
                        pltpu.VMEM((2, SH_PAIR, 2, SH_TK, SH_TQ), BF16),
                        pltpu.VMEM((3, SH_TK, SH_TQ), F32)],
        compiler_params=_cparams(("arbitrary", "arbitrary")),
        name="dilated",
    )(q, k, v)


FFT_N2 = 128


def _fft1_body(zr_ref, zi_ref, f_ref, twc_ref, tws_ref, yr_ref, yi_ref, *, n1):
    zz = jnp.concatenate([zr_ref[...], zi_ref[...]], axis=0)
    y = _dot(f_ref[...], zz)
    yr, yi = y[:n1], y[n1:]
    c, s = twc_ref[...], tws_ref[...]
    yr_ref[...] = (yr * c + yi * s).astype(BF16)
    yi_ref[...] = (yi * c - yr * s).astype(BF16)


def _fft1(zr, zi, fmat, twc, tws, *, wb):
    b, n1, w = zr.shape
    dat = pl.BlockSpec((None, n1, wb), lambda bi, j: (bi, 0, j))
    tw = pl.BlockSpec((n1, wb), lambda bi, j: (0, j))
    return pl.pallas_call(
        functools.partial(_fft1_body, n1=n1),
        grid=(b, w // wb),
        in_specs=[dat, dat, _resident(fmat.shape), tw, tw],
        out_specs=[dat, dat],
        out_shape=[jax.ShapeDtypeStruct((b, n1, w), BF16)] * 2,
        compiler_params=_cparams(("arbitrary", "arbitrary")),
        name="fft_stage1",
    )(zr, zi, fmat, twc, tws)


def _fft2_body(yr_ref, yi_ref, f_ref, o_ref, *, kb):
    f = f_ref[...]
    for j in range(kb):
        rs = pl.ds(j * FFT_N2, FFT_N2)
        yy = jnp.concatenate([yr_ref[rs, :], yi_ref[rs, :]], axis=0)
        o_ref[:, j * B_WIDTH:(j + 1) * B_WIDTH] = _dot(f, yy).astype(o_ref.dtype)


def _fft2(yr, yi, fmat, *, kb):
    b, rows, w = yr.shape
    n1 = rows // FFT_N2
    dat = pl.BlockSpec((None, kb * FFT_N2, w), lambda bi, j: (bi, j, 0))
    return pl.pallas_call(
        functools.partial(_fft2_body, kb=kb),
        grid=(b, n1 // kb),
        in_specs=[dat, dat, _resident(fmat.shape)],
        out_specs=pl.BlockSpec((None, FFT_N2, kb * w), lambda bi, j: (bi, 0, j)),
        out_shape=jax.ShapeDtypeStruct((b, FFT_N2, n1 * w), BF16),
        compiler_params=_cparams(("arbitrary", "arbitrary")),
        name="fft_stage2",
    )(yr, yi, fmat)


def _odd_in_body(x_ref, g_ref, w_ref, gql_ref, wq_ref, gkvl_ref, wkv_ref, gq_ref, gk_ref,
                 cos_ref, sin_ref, q_ref, k_ref, v_ref, u_ref):
    h = _rms_rows(x_ref[...], g_ref[...]).astype(BF16)
    z = _dot(h, w_ref[...])
    cos, sin = cos_ref[...], sin_ref[...]
    kv_off = C_Q_RANK
    pe_off = kv_off + C_KV_RANK
    rot_off = pe_off + LANES
    u_off = rot_off + LANES
    hw = N_HEADS_C * LANES
    q_lat = _rms_rows(z[:, :C_Q_RANK], gql_ref[...]).astype(BF16)
    kv_lat = _rms_rows(z[:, kv_off:pe_off], gkvl_ref[...]).astype(BF16)
    k_pe = z[:, pe_off:rot_off]
    k_pe_rot = z[:, rot_off:u_off]
    qf = _dot(q_lat, wq_ref[...])
    kvf = _dot(kv_lat, wkv_ref[...])

    def head(t, t_rot, gains, scale):
        ms = jnp.sum(t * t, axis=-1, keepdims=True) * (1.0 / C_QK_DIM)
        r = lax.rsqrt(ms + NORM_EPS) * scale
        return (t * (gains[0:1] * cos) + t_rot * (gains[1:2] * sin)) * r

    gq, gk = gq_ref[...], gk_ref[...]
    for hd in range(N_HEADS_C):
        sl = slice(hd * LANES, (hd + 1) * LANES)
        rot_sl = slice(hw + hd * LANES, hw + (hd + 1) * LANES)
        q_ref[:, sl] = head(qf[:, sl], qf[:, rot_sl], gq,
                            C_QK_DIM ** -0.5 * math.log2(math.e)).astype(BF16)
        k_ref[:, sl] = head(kvf[:, sl] + k_pe, k_pe_rot, gk, 1.0).astype(BF16)
    v_ref[...] = kvf[:, hw:].astype(BF16)
    u_ref[...] = z[:, u_off:]


def _odd_in(x, g, w_in, gql, wq, gkvl, wkv, gq, gk, cos, sin, *, ts):
    b, s, d = x.shape
    tok = lambda w: pl.BlockSpec((None, ts, w), lambda bi, i: (bi, i, 0))
    hw = N_HEADS_C * LANES
    return pl.pallas_call(
        _odd_in_body,
        grid=(b, s // ts),
        in_specs=[tok(d), _resident((1, d)), _resident(w_in.shape), _resident(gql.shape),
                  _resident(wq.shape), _resident(gkvl.shape), _resident(wkv.shape),
                  _resident(gq.shape), _resident(gk.shape), tok(LANES), tok(LANES)],
        out_specs=[tok(hw), tok(hw), tok(N_HEADS_C * C_V_DIM),
                   pl.BlockSpec((ts, D_WIDTH), lambda bi, i: (i, bi))],
        out_shape=[jax.ShapeDtypeStruct((b, s, hw), BF16), jax.ShapeDtypeStruct((b, s, hw), BF16),
                   jax.ShapeDtypeStruct((b, s, N_HEADS_C * C_V_DIM), BF16),
                   jax.ShapeDtypeStruct((s, b * D_WIDTH), F32)],
        compiler_params=_cparams(("arbitrary", "arbitrary")),
        name="odd_in",
    )(x, g, w_in, gql, wq, gkvl, wkv, gq, gk, cos, sin)


FLASH_ONES_ROWS = 16


def _flash_body(q_ref, k_ref, vt_ref, o_ref, s_scr, p_scr, *, tq, tk):
    n_q = q_ref.shape[0] // tq
    n_kv = k_ref.shape[0] // tk
    heads = ((0, 0), (LANES, C_V_DIM))
    ones = jnp.ones((FLASH_ONES_ROWS, tk), BF16)
    items = [(qi, kj) for qi in range(n_q) for kj in range(n_kv)]

    def scores(item, slot):
        qi, kj = item
        col_max = []
        for h, (off, _) in enumerate(heads):
            s = lax.dot_general(k_ref[kj * tk:(kj + 1) * tk, off:off + LANES],
                                q_ref[qi * tq:(qi + 1) * tq, off:off + LANES],
                                (((1,), (1,)), ((), ())), preferred_element_type=F32)
            s_scr[slot, h] = s
            col_max.append(jnp.max(s, axis=0, keepdims=True))
        return col_max

    def softmax(slot, m2, col_max2):
        m_out, alpha_out = [], []
        for h in range(2):
            m_new = jnp.maximum(m2[h], col_max2[h])
            p_scr[slot, h] = jnp.exp2(s_scr[slot, h] - m_new).astype(BF16)
            alpha_out.append(jnp.exp2(m2[h] - m_new))
            m_out.append(m_new)
        return m_out, alpha_out

    def accumulate(item, slot, acc2, alpha2):
        _, kj = item
        out = []
        for h, (_, voff) in enumerate(heads):
            vt = jnp.concatenate([vt_ref[voff:voff + C_V_DIM, kj * tk:(kj + 1) * tk], ones], axis=0)
            out.append(acc2[h] * alpha2[h] + _dot(vt, p_scr[slot, h]))
        return out

    m0 = jnp.full((1, tq), NEG_INF, F32)
    a0 = jnp.zeros((C_V_DIM + FLASH_ONES_ROWS, tq), F32)
    m_state = [[m0, m0] for _ in range(n_q)]
    acc_state = [[a0, a0] for _ in range(n_q)]
    col, alpha = {}, {}
    for t in range(len(items) + 2):
        if t < len(items):
            col[t] = scores(items[t], t % 2)
        if 0 <= t - 1 < len(items):
            qi = items[t - 1][0]
            m_state[qi], alpha[t - 1] = softmax((t - 1) % 2, m_state[qi], col.pop(t - 1))
        if 0 <= t - 2 < len(items):
            qi, kj = items[t - 2]
            acc_state[qi] = accumulate(items[t - 2], t % 2, acc_state[qi], alpha.pop(t - 2))
            if kj == n_kv - 1:
                out_t = jnp.concatenate(
                    [acc[:C_V_DIM] / acc[C_V_DIM:C_V_DIM + 1] for acc in acc_state[qi]], axis=0)
                o_ref[qi * tq:(qi + 1) * tq, :] = out_t.T.astype(o_ref.dtype)


def _flash(q, k, vt, *, tq, tk, n_q):
    b, s, hw = q.shape
    pairs = hw // (2 * LANES)
    bq = tq * n_q
    assert s % bq == 0 and s % tk == 0
    return pl.pallas_call(
        functools.partial(_flash_body, tq=tq, tk=tk),
        grid=(b, pairs, s // bq),
        in_specs=[pl.BlockSpec((None, bq, 2 * LANES), lambda bi, p, i: (bi, i, p)),
                  pl.BlockSpec((None, s, 2 * LANES), lambda bi, p, i: (bi, 0, p)),
                  pl.BlockSpec((None, 2 * C_V_DIM, s), lambda bi, p, i: (bi, p, 0))],
        out_specs=pl.BlockSpec((None, bq, LANES), lambda bi, p, i: (bi, i, p)),
        out_shape=jax.ShapeDtypeStruct((b, s, pairs * LANES), BF16),
        scratch_shapes=[pltpu.VMEM((2, 2, tk, tq), F32), pltpu.VMEM((2, 2, tk, tq), BF16)],
        compiler_params=_cparams(("arbitrary", "arbitrary", "arbitrary")),
        name="flash",
    )(q, k, vt)


def _s5_body(u_ref, bm_ref, cm_ref, lam_ref, y_ref, bu_scr, xs_scr, st_scr, *, tc, nb):
    direction = pl.program_id(0)

    @pl.when(pl.program_id(1) == 0)
    def _():
        st_scr[...] = jnp.zeros(st_scr.shape, F32)

    bu_scr[...] = _dot(u_ref[...].astype(BF16), bm_ref[...])
    a_re = jnp.broadcast_to(lam_ref[0:1, :], (nb, S5_LANES))
    a_im = jnp.broadcast_to(lam_ref[1:2, :], (nb, S5_LANES))

    def step(t, carry):
        x_re, x_im = carry
        tt = jnp.where(direction == 0, t, tc - 1 - t)
        rs = pl.ds(pl.multiple_of(tt * nb, nb), nb)
        n_re = a_re * x_re - a_im * x_im + bu_scr[rs, :S5_LANES]
        n_im = a_re * x_im + a_im * x_re + bu_scr[rs, S5_LANES:]
        xs_scr[rs, :S5_LANES] = n_re
        xs_scr[rs, S5_LANES:] = n_im
        return n_re, n_im

    x_re, x_im = lax.fori_loop(0, tc, step, (st_scr[:, :S5_LANES], st_scr[:, S5_LANES:]),
                               unroll=4)
    st_scr[:, :S5_LANES] = x_re
    st_scr[:, S5_LANES:] = x_im
    y_ref[...] = _dot(xs_scr[...].astype(BF16), cm_ref[...])


def _s5_scan(u2, bmat, cmat, lam, *, tc, nb):
    rows, dw = u2.shape
    n_chunks = rows // (tc * nb)

    def chunk(d, c):
        return jnp.where(d == 0, c, n_chunks - 1 - c)

    return pl.pallas_call(
        functools.partial(_s5_body, tc=tc, nb=nb),
        grid=(2, n_chunks),
        in_specs=[pl.BlockSpec((tc * nb, dw), lambda d, c: (chunk(d, c), 0)),
                  pl.BlockSpec((None, dw, 2 * S5_LANES), lambda d, c: (d, 0, 0)),
                  pl.BlockSpec((None, 2 * S5_LANES, dw), lambda d, c: (d, 0, 0)),
                  pl.BlockSpec((None, 2, S5_LANES), lambda d, c: (d, 0, 0))],
        out_specs=pl.BlockSpec((None, tc * nb, dw), lambda d, c: (d, chunk(d, c), 0)),
        out_shape=jax.ShapeDtypeStruct((2, rows, dw), F32),
        scratch_shapes=[pltpu.VMEM((tc * nb, 2 * S5_LANES), F32),
                        pltpu.VMEM((tc * nb, 2 * S5_LANES), F32),
                        pltpu.VMEM((nb, 2 * S5_LANES), F32)],
        compiler_params=_cparams(("arbitrary", "arbitrary")),
        name="s5_scan",
    )(u2, bmat, cmat, lam)


def _s5_glu_body(y_ref, u_ref, dskip_ref, w_ref, b_ref, o_ref):
    y = y_ref[0] + y_ref[1] + u_ref[...] * dskip_ref[...]
    z = jax.nn.gelu(y)
    gate = jax.nn.sigmoid(_dot(z.astype(BF16), w_ref[...]) + b_ref[...])
    o_ref[...] = (z * gate).astype(o_ref.dtype)


def _s5_glu(y, u2, dskip, w, bias, *, tr):
    rows, dw = u2.shape
    return pl.pallas_call(
        _s5_glu_body,
        grid=(rows // tr,),
        in_specs=[pl.BlockSpec((2, tr, dw), lambda i: (0, i, 0)),
                  pl.BlockSpec((tr, dw), lambda i: (i, 0)),
                  _resident((1, dw)), _resident(w.shape), _resident((1, dw))],
        out_specs=pl.BlockSpec((tr, dw), lambda i: (i, 0)),
        out_shape=jax.ShapeDtypeStruct((rows, dw), BF16),
        compiler_params=_cparams(("arbitrary",)),
        name="s5_glu",
    )(y, u2, dskip, w, bias)


def _rotary_tables(positions, rot_dim, lane_of_first, period):
    half = rot_dim // 2
    inv_freq = ROPE_THETA ** (-jnp.arange(0, rot_dim, 2, dtype=F32) / rot_dim)
    ang = positions.astype(F32)[..., None] * inv_freq
    ang = jnp.concatenate([ang, ang], axis=-1)
    cos_r, sin_r = jnp.cos(ang), jnp.sin(ang)
    lane = np.arange(LANES)
    j = (lane - lane_of_first) % period
    in_rot = (lane >= lane_of_first) & (j < rot_dim)
    idx = np.where(in_rot, j, 0)
    cos = jnp.where(in_rot, cos_r[..., idx], 1.0)
    sin = jnp.where(in_rot, sin_r[..., idx], 0.0)
    s_up = jnp.where(in_rot & (j < half), -sin, 0.0)
    s_dn = jnp.where(in_rot & (j >= half), sin, 0.0)
    return cos, s_up, s_dn


def _dft_mats(n):
    ang = 2.0 * np.pi * np.outer(np.arange(n), np.arange(n)) / n
    return np.cos(ang), np.sin(ang)


def _block_diag(blocks):
    g, r, c = blocks.shape
    eye = jnp.eye(g, dtype=blocks.dtype)
    return jnp.einsum("grc,gh->grhc", blocks, eye).reshape(g * r, g * c)


def _pick_tile(n, pref):
    t = min(n, pref)
    while n % t:
        t //= 2
    return t


def kernel(x, positions, ffn1_norm, ffn1_w_gate, ffn1_w_up, ffn1_w_down, mix_norm, ffn2_norm, ffn2_w_gate, ffn2_w_up, ffn2_w_down, ab_w_in, ab_w_out, a_q_norm, a_k_norm, b_w_mix, cd_w_in, cd_w_out, c_q_lat_norm, c_w_q_up, c_kv_lat_norm, c_w_kv_up, c_q_norm, c_k_norm, d_lam_re, d_lam_im, d_log_step, d_b_re, d_b_im, d_c_re, d_c_im, d_skip, d_w_glu, d_b_glu):
    bsz, seq, d_model = x.shape
    depth = ffn1_norm.shape[0]
    tokens = bsz * seq
    ts = _pick_tile(seq, 512)
    assert bsz % SUBLANES == 0
    tm = _pick_tile(seq, 512)
    n1 = seq // FFT_N2
    row = lambda v: v.reshape(1, -1).astype(F32)

    cos_a, sup_a, sdn_a = _rotary_tables(positions, ROT_DIM_A, 0, HEAD_DIM)
    cos_c, sup_c, sdn_c = _rotary_tables(positions, C_ROPE_DIM, C_NOPE_DIM, LANES)

    cc, sc_ = _dft_mats(FNET_GROUP_DIM)
    norm = 1.0 / math.sqrt(seq * FNET_GROUP_DIM)
    eye_g = np.eye(FNET_GROUPS)
    dcos = jnp.asarray(np.kron(eye_g, cc) * norm, BF16)
    dsin = jnp.asarray(np.kron(eye_g, -sc_) * norm, BF16)
    c1, s1 = _dft_mats(n1)
    f1 = jnp.asarray(np.block([[c1, s1], [-s1, c1]]), BF16)
    c2, s2 = _dft_mats(FFT_N2)
    f2 = jnp.asarray(np.concatenate([c2, s2], axis=1), BF16)
    tw_ang = 2.0 * np.pi * np.outer(np.arange(n1), np.arange(FFT_N2)) / seq
    twc = jnp.repeat(jnp.asarray(np.cos(tw_ang), F32), B_WIDTH, axis=1)
    tws = jnp.repeat(jnp.asarray(np.sin(tw_ang), F32), B_WIDTH, axis=1)
    head_mean = jnp.asarray(np.kron(np.eye(N_HEADS_A), np.full((HEAD_DIM, HEAD_DIM), 1.0 / HEAD_DIM)), BF16)

    x2 = x.reshape(tokens, d_model)
    for layer in range(depth):
        i = layer // 2
        x2 = _ffn(x2, row(ffn1_norm[layer]), ffn1_w_gate[layer].astype(BF16),
                  ffn1_w_up[layer].astype(BF16), ffn1_w_down[layer].astype(BF16), tm=tm)
        x3 = x2.reshape(bsz, seq, d_model)
        if layer % 2 == 0:
            mbd = _block_diag(b_w_mix[i]).astype(BF16)
            q, k, v, zr, zi = _even_in(
                x3, row(mix_norm[layer]), ab_w_in[i].astype(BF16), head_mean,
                row(jnp.tile(a_q_norm[i], N_HEADS_A)), row(jnp.tile(a_k_norm[i], N_HEADS_A)),
                cos_a, sup_a, sdn_a, dcos, dsin, mbd, ts=ts)
            a_out = _dilated_sh(q, k, v)
            shp = (bsz, n1, FFT_N2 * B_WIDTH)
            yr, yi = _fft1(zr.reshape(shp), zi.reshape(shp), f1, twc, tws,
                           wb=_pick_tile(FFT_N2 * B_WIDTH, 4096))
            shp2 = (bsz, n1 * FFT_N2, B_WIDTH)
            spec = _fft2(yr.reshape(shp2), yi.reshape(shp2), f2, kb=_pick_tile(n1, 8))
            w_out = ab_w_out[i].astype(BF16)
            mixer = (a_out.reshape(tokens, A_WIDTH), spec.reshape(tokens, B_WIDTH),
                     w_out[:A_WIDTH], w_out[A_WIDTH:],
                     pl.BlockSpec((tm, B_WIDTH), lambda r: (r, 0)))
        else:
            half = C_ROPE_DIM // 2
            rot_cols = lambda w: jnp.concatenate([-w[..., half:], w[..., :half]], axis=-1)
            swap_halves = lambda w: jnp.concatenate([w[..., half:], w[..., :half]], axis=-1)
            rope_slab = lambda w: jnp.pad(
                w, [(0, 0)] * (w.ndim - 1) + [(C_NOPE_DIM, LANES - C_QK_DIM)])
            w_in = cd_w_in[i]
            pe_off = C_Q_RANK + C_KV_RANK
            w_pe = w_in[:, pe_off:pe_off + C_ROPE_DIM]
            w_in_p = jnp.concatenate([w_in[:, :pe_off], rope_slab(w_pe), rope_slab(rot_cols(w_pe)),
                                      w_in[:, pe_off + C_ROPE_DIM:]], axis=1).astype(BF16)
            pad_h = lambda w, keep: jnp.pad(
                w, ((0, 0), (0, 0), (0, LANES - keep))).reshape(w.shape[0], N_HEADS_C * LANES)
            wq = c_w_q_up[i].reshape(C_Q_RANK, N_HEADS_C, C_QK_DIM)
            wq_p = jnp.concatenate(
                [pad_h(wq, C_QK_DIM),
                 rope_slab(rot_cols(wq[..., C_NOPE_DIM:])).reshape(C_Q_RANK, N_HEADS_C * LANES)],
                axis=1).astype(BF16)
            wkv = c_w_kv_up[i].reshape(C_KV_RANK, N_HEADS_C, C_NOPE_DIM + C_V_DIM)
            wkv_p = jnp.concatenate(
                [pad_h(wkv[:, :, :C_NOPE_DIM], C_NOPE_DIM),
                 wkv[:, :, C_NOPE_DIM:].reshape(C_KV_RANK, N_HEADS_C * C_V_DIM)], axis=1).astype(BF16)
            gains = lambda gvec: jnp.stack(
                [jnp.pad(gvec, (0, LANES - C_QK_DIM)),
                 rope_slab(swap_halves(gvec[C_NOPE_DIM:]))]).astype(F32)
            q, k, v, u_t = _odd_in(
                x3, row(mix_norm[layer]), w_in_p, row(c_q_lat_norm[i]), wq_p,
                row(c_kv_lat_norm[i]), wkv_p, gains(c_q_norm[i]), gains(c_k_norm[i]),
                cos_c, sdn_c - sup_c, ts=ts)
            c_out = _flash(q, k, jnp.swapaxes(v, 1, 2), tq=_pick_tile(seq, 256),
                           tk=_pick_tile(seq, 256), n_q=2)

            lam = lax.complex(d_lam_re[i].astype(F32), d_lam_im[i].astype(F32))
            step = jnp.exp(d_log_step[i].astype(F32))[..., None]
            lam_bar = jnp.exp(lam * step)
            b_bar = ((lam_bar - 1.0) / lam)[..., None] * lax.complex(
                d_b_re[i].astype(F32), d_b_im[i].astype(F32))
            to_bd = lambda m: jnp.stack([_block_diag(m[d]) for d in range(2)])
            b_t = jnp.swapaxes(b_bar, -1, -2)
            bmat = jnp.concatenate([to_bd(jnp.real(b_t)), to_bd(jnp.imag(b_t))], axis=-1).astype(BF16)
            c_re_t = jnp.swapaxes(d_c_re[i].astype(F32), -1, -2)
            c_im_t = jnp.swapaxes(d_c_im[i].astype(F32), -1, -2)
            cmat = jnp.concatenate([to_bd(c_re_t), to_bd(-c_im_t)], axis=-2).astype(BF16)
            lam2 = jnp.stack([jnp.real(lam_bar).reshape(2, S5_LANES),
                              jnp.imag(lam_bar).reshape(2, S5_LANES)], axis=1)

            u2 = u_t.reshape(seq * bsz, D_WIDTH)
            tc = _pick_tile(seq, 128)
            y = _s5_scan(u2, bmat, cmat, lam2, tc=tc, nb=bsz)
            d2 = _s5_glu(y, u2, row(d_skip[i]), d_w_glu[i].astype(BF16), row(d_b_glu[i]),
                         tr=_pick_tile(seq * bsz, 1024))
            d_out = d2.reshape(seq, bsz * D_WIDTH)
            w_out = cd_w_out[i].astype(BF16)
            tiles_per_seq = seq // tm
            mixer = (c_out.reshape(tokens, N_HEADS_C * C_V_DIM), d_out,
                     w_out[:N_HEADS_C * C_V_DIM], w_out[N_HEADS_C * C_V_DIM:],
                     pl.BlockSpec((tm, D_WIDTH),
                                  lambda r: (r % tiles_per_seq, r // tiles_per_seq)))
        x2 = _ffn(x2, row(ffn2_norm[layer]), ffn2_w_gate[layer].astype(BF16),
                  ffn2_w_up[layer].astype(BF16), ffn2_w_down[layer].astype(BF16), tm=tm,
                  mixer=mixer)
    return x2.reshape(bsz, seq, d_model)
```

```python
import functools
import math

import jax
import jax.numpy as jnp
import numpy as np
from jax import lax
from jax.experimental import pallas as pl
from jax.experimental.pallas import tpu as pltpu

F32 = jnp.float32
BF16 = jnp.bfloat16

HEAD_DIM = 64
N_HEADS_A = 8
ROT_DIM_A = 16
DILATIONS = (1, 4, 16)
RADIUS = 64
FNET_GROUPS = 4
FNET_GROUP_DIM = 64
N_HEADS_C = 8
C_NOPE_DIM = 64
C_ROPE_DIM = 32
C_V_DIM = 64
C_QK_DIM = C_NOPE_DIM + C_ROPE_DIM
C_Q_RANK = 256
C_KV_RANK = 128
S5_GROUPS = 16
S5_GROUP_DIM = 16
S5_STATE = 64
ROPE_THETA = 500000.0
NORM_EPS = 1e-6
NEG_INF = -1e30
A_WIDTH = N_HEADS_A * HEAD_DIM
B_WIDTH = FNET_GROUPS * FNET_GROUP_DIM
D_WIDTH = S5_GROUPS * S5_GROUP_DIM
S5_LANES = S5_GROUPS * S5_STATE
S5_BLOCK_ROWS = 128

LANES = 128
SUBLANES = 8
VMEM_LIMIT_BYTES = 56 * 1024 * 1024


def _cparams(sem):
    return pltpu.CompilerParams(dimension_semantics=sem, vmem_limit_bytes=VMEM_LIMIT_BYTES)


def _resident(shape):
    nd = len(shape)
    return pl.BlockSpec(shape, lambda *_: (0,) * nd, pipeline_mode=pl.Buffered(1))


def _rms_rows(x, g):
    ms = jnp.mean(x * x, axis=-1, keepdims=True)
    return x * lax.rsqrt(ms + NORM_EPS) * g


def _dot(a, b):
    return jnp.dot(a, b, preferred_element_type=F32)


FFN_CHUNK = 256


def _ffn_body(*refs, n_chunks, fused_mixer):
    if fused_mixer:
        x_ref, a_ref, b_ref, wa_ref, wb_ref, g_ref, wg_ref, wu_ref, wd_ref, o_ref = refs
        x = (x_ref[...] + _dot(a_ref[...], wa_ref[...])
             + _dot(b_ref[...].astype(BF16), wb_ref[...]))
    else:
        x_ref, g_ref, wg_ref, wu_ref, wd_ref, o_ref = refs
        x = x_ref[...]
    h = _rms_rows(x, g_ref[...]).astype(BF16)
    acc = jnp.zeros(x.shape, F32)
    for c in range(n_chunks):
        sl = pl.ds(c * FFN_CHUNK, FFN_CHUNK)
        gate = _dot(h, wg_ref[:, sl])
        up = _dot(h, wu_ref[:, sl])
        act = (gate * jax.nn.sigmoid(gate) * up).astype(BF16)
        acc = acc + _dot(act, wd_ref[sl, :])
    o_ref[...] = x + 0.5 * acc


def _ffn(x2, g, wg, wu, wd, *, tm, mixer=None):
    t, d = x2.shape
    f = wg.shape[1]
    assert t % tm == 0 and f % FFN_CHUNK == 0
    rows = lambda w: pl.BlockSpec((tm, w), lambda i: (i, 0))
    weights = [_resident((1, d)), _resident((d, f)), _resident((d, f)), _resident((f, d))]
    if mixer is None:
        in_specs, args = [rows(d)] + weights, (x2, g, wg, wu, wd)
    else:
        a, b, wa, wb, b_spec = mixer
        in_specs = [rows(d), rows(a.shape[1]), b_spec, _resident(wa.shape), _resident(wb.shape)] + weights
        args = (x2, a, b, wa, wb, g, wg, wu, wd)
    return pl.pallas_call(
        functools.partial(_ffn_body, n_chunks=f // FFN_CHUNK, fused_mixer=mixer is not None),
        grid=(t // tm,),
        in_specs=in_specs,
        out_specs=rows(d),
        out_shape=jax.ShapeDtypeStruct((t, d), F32),
        compiler_params=_cparams(("arbitrary",)),
        name="ffn_mix" if mixer is not None else "ffn",
    )(*args)


def _rotary_block(t, cos, s_up, s_dn, shift):
    return (t * cos + pltpu.roll(t, LANES - shift, axis=1) * s_up
            + pltpu.roll(t, shift, axis=1) * s_dn)


def _even_in_body(x_ref, g_ref, w_ref, pm_ref, gq_ref, gk_ref, cos_ref, sup_ref, sdn_ref,
                  dcos_ref, dsin_ref, mbd_ref,
                  q_ref, k_ref, v_ref, zr_ref, zi_ref, wc_scr):
    first = jnp.logical_and(pl.program_id(0) == 0, pl.program_id(1) == 0)

    @pl.when(first)
    def _():
        mbd = mbd_ref[...]
        wc_scr[:, :B_WIDTH] = _dot(dcos_ref[...], mbd).astype(BF16)
        wc_scr[:, B_WIDTH:] = _dot(dsin_ref[...], mbd).astype(BF16)

    h = _rms_rows(x_ref[...], g_ref[...]).astype(BF16)
    z = _dot(h, w_ref[...])
    cos, s_up, s_dn = cos_ref[...], sup_ref[...], sdn_ref[...]
    pm = pm_ref[...]

    def head_norm_rot(t, gain, scale):
        ms = _dot((t * t).astype(BF16), pm)
        tn = t * lax.rsqrt(ms + NORM_EPS) * gain
        blocks = [_rotary_block(tn[:, c * LANES:(c + 1) * LANES], cos, s_up, s_dn, ROT_DIM_A // 2)
                  for c in range(A_WIDTH // LANES)]
        out = jnp.concatenate(blocks, axis=1)
        return out * scale if scale != 1.0 else out

    q_ref[...] = head_norm_rot(z[:, :A_WIDTH], gq_ref[...], HEAD_DIM ** -0.5 * math.log2(math.e))
    k_ref[...] = head_norm_rot(z[:, A_WIDTH:2 * A_WIDTH], gk_ref[...], 1.0)
    v_ref[...] = z[:, 2 * A_WIDTH:3 * A_WIDTH]
    zz = _dot(z[:, 3 * A_WIDTH:].astype(BF16), wc_scr[...])
    zr_ref[...] = zz[:, :B_WIDTH]
    zi_ref[...] = zz[:, B_WIDTH:]


def _even_in(x, g, w_in, pm, gq, gk, cos, s_up, s_dn, dcos, dsin, mbd, *, ts):
    b, s, d = x.shape
    tok = lambda w: pl.BlockSpec((None, ts, w), lambda bi, i: (bi, i, 0))
    return pl.pallas_call(
        _even_in_body,
        grid=(b, s // ts),
        in_specs=[tok(d), _resident((1, d)), _resident(w_in.shape), _resident(pm.shape),
                  _resident((1, A_WIDTH)), _resident((1, A_WIDTH)),
                  tok(LANES), tok(LANES), tok(LANES),
                  _resident(dcos.shape), _resident(dsin.shape), _resident(mbd.shape)],
        out_specs=[tok(A_WIDTH), tok(A_WIDTH), tok(A_WIDTH), tok(B_WIDTH), tok(B_WIDTH)],
        out_shape=[jax.ShapeDtypeStruct((b, s, A_WIDTH), F32)] * 3
        + [jax.ShapeDtypeStruct((b, s, B_WIDTH), F32)] * 2,
        scratch_shapes=[pltpu.VMEM((B_WIDTH, 2 * B_WIDTH), BF16)],
        compiler_params=_cparams(("arbitrary", "arbitrary")),
        name="even_in",
    )(x, g, w_in, pm, gq, gk, cos, s_up, s_dn, dcos, dsin, mbd)


DIL_ONES_ROWS = 16
SH_TQ = LANES
SH_TK = 2 * LANES
SH_QPAD = RADIUS
SH_KPAD = LANES
SH_PAIR = 4


def _dilated_sh_body(q_ref, k_ref, v_ref, o_ref, qd, kd, vt, o_st, l_st, out_scr, s_scr, p_scr,
                     bias_scr, *, seq):
    lane = lax.broadcasted_iota(jnp.int32, (1, LANES), 1)
    head_lanes = (lane < HEAD_DIM, lane >= HEAD_DIM)
    kk = lax.broadcasted_iota(jnp.int32, (SH_TK, SH_TQ), 0)
    qq = lax.broadcasted_iota(jnp.int32, (SH_TK, SH_TQ), 1)
    band = jnp.abs(kk - qq - (SH_KPAD - SH_QPAD)) <= RADIUS
    key_ok = (kk >= SH_KPAD, kk >= 0, kk < SH_KPAD)
    for idx in range(3):
        bias_scr[idx] = jnp.where(jnp.logical_and(band, key_ok[idx]), 0.0, NEG_INF)
    ones = jnp.ones((DIL_ONES_ROWS, SH_TK), BF16)
    max_dil = max(DILATIONS)
    state_pad = SH_QPAD * max_dil

    order = tuple(reversed(DILATIONS))
    for dil in order:
        first, last = dil == order[0], dil == order[-1]
        sub_len = seq // dil
        chunks = sub_len // LANES
        tiles_per_class = chunks + 1
        n_tiles = dil * tiles_per_class
        n_items = -(-n_tiles // SH_PAIR)
        assert last or n_tiles % SH_PAIR == 0
        q_stride = sub_len + 2 * SH_QPAD
        k_stride = sub_len + 2 * SH_KPAD

        def zero_pads(res, carry, sub_len=sub_len, q_stride=q_stride, k_stride=k_stride):
            zq = jnp.zeros((SH_QPAD, LANES), BF16)
            zk = jnp.zeros((SH_KPAD, LANES), BF16)
            for off in (0, SH_QPAD + sub_len):
                qd[pl.ds(pl.multiple_of(res * q_stride + off, SH_QPAD), SH_QPAD), :] = zq
            for off in (0, SH_KPAD + sub_len):
                pad = pl.ds(pl.multiple_of(res * k_stride + off, SH_KPAD), SH_KPAD)
                kd[pad, :] = zk
                vt[:, pad] = jnp.zeros((LANES, SH_KPAD), BF16)
            return carry

        lax.fori_loop(0, dil, zero_pads, 0)

        def prepare(idx, carry, dil=dil, chunks=chunks, q_stride=q_stride, k_stride=k_stride):
            res = idx // chunks
            c0 = (idx - res * chunks) * LANES
            if dil == 1:
                src = pl.ds(pl.multiple_of(c0, LANES), LANES)
            else:
                src = pl.ds(res + dil * c0, LANES, stride=dil)
            k_dst = pl.ds(pl.multiple_of(res * k_stride + SH_KPAD + c0, LANES), LANES)
            qd[pl.ds(pl.multiple_of(res * q_stride + SH_QPAD + c0, SH_QPAD), LANES), :] = (
                q_ref[src, :].astype(BF16))
            kd[k_dst, :] = k_ref[src, :].astype(BF16)
            vt[:, k_dst] = v_ref[src, :].T.astype(BF16)
            return carry

        lax.fori_loop(0, seq // LANES, prepare, 0, unroll=4)

        def geometry(tile, tiles_per_class=tiles_per_class, n_tiles=n_tiles):
            tile = jnp.minimum(tile, n_tiles - 1)
            res = tile // tiles_per_class
            i = tile - res * tiles_per_class
            edge = (i > 0).astype(jnp.int32) + (i == tiles_per_class - 1).astype(jnp.int32)
            return res, i, edge

        def scores(item, slot, geometry=geometry, q_stride=q_stride, k_stride=k_stride):
            col_max = []
            for u in range(SH_PAIR):
                res, i, edge = geometry(item * SH_PAIR + u)
                k_win = kd[pl.ds(pl.multiple_of(res * k_stride + i * LANES, LANES), SH_TK), :]
                q2 = qd[pl.ds(pl.multiple_of(res * q_stride + i * LANES, LANES), SH_TQ), :]
                bias = bias_scr[edge]
                for h in range(2):
                    qh = jnp.where(head_lanes[h], q2, jnp.zeros_like(q2))
                    s = lax.dot_general(k_win, qh, (((1,), (1,)), ((), ())),
                                        preferred_element_type=F32) + bias
                    s_scr[slot, u, h] = s
                    col_max.append(jnp.max(s, axis=0, keepdims=True))
            return tuple(col_max)

        def exponentials(slot, col_max):
            for u in range(SH_PAIR):
                for h in range(2):
                    p_scr[slot, u, h] = jnp.exp2(s_scr[slot, u, h] - col_max[2 * u + h]).astype(BF16)

        def values(item, slot, col_max, geometry=geometry, dil=dil, k_stride=k_stride,
                   first=first, last=last):
            for u in range(SH_PAIR):
                res, i, _ = geometry(item * SH_PAIR + u)
                cols = pl.ds(pl.multiple_of(res * k_stride + i * LANES, LANES), SH_TK)
                o_t, l_t = [], []
                for h in range(2):
                    vth = jnp.concatenate([vt[h * HEAD_DIM:(h + 1) * HEAD_DIM, cols], ones], axis=0)
                    acc = _dot(vth, p_scr[slot, u, h])
                    denom = acc[HEAD_DIM:HEAD_DIM + 1]
                    o_t.append(acc[:HEAD_DIM] / denom)
                    l_t.append(jnp.broadcast_to(col_max[2 * u + h] + jnp.log2(denom),
                                                (HEAD_DIM, SH_TQ)))
                o_new = jnp.concatenate(o_t, axis=0).T
                l_new = jnp.concatenate(l_t, axis=0).T
                start = state_pad + res + dil * (i * LANES - SH_QPAD)
                if dil == 1:
                    st_rows = pl.ds(pl.multiple_of(start, SH_QPAD), SH_TQ)
                else:
                    st_rows = pl.ds(start, SH_TQ, stride=dil)
                if not first:
                    o_old, l_old = o_st[st_rows, :], l_st[st_rows, :]
                    l_max = jnp.maximum(l_old, l_new)
                    w_old, w_new = jnp.exp2(l_old - l_max), jnp.exp2(l_new - l_max)
                    w_sum = w_old + w_new
                    o_new = (w_old * o_old + w_new * o_new) / w_sum
                    l_new = l_max + jnp.log2(w_sum)
                if last:
                    out_scr[pl.ds(pl.multiple_of(i * LANES, LANES), SH_TQ), :] = o_new.astype(BF16)
                else:
                    o_st[st_rows, :] = o_new
                    l_st[st_rows, :] = l_new

        def tick(t, parity, carry, scores=scores, exponentials=exponentials, values=values):
            cm_prev, cm_prev2 = carry
            cm_new = scores(t, parity)
            exponentials(1 - parity, cm_prev)
            values(t - 2, parity, cm_prev2)
            return cm_new, cm_prev

        cm0 = scores(0, 0)
        cm1 = scores(1, 1)
        exponentials(0, cm0)

        def two_ticks(j, carry, tick=tick):
            t = 2 * j + 2
            return tick(t + 1, 1, tick(t, 0, carry))

        loops = (n_items - 2) // 2
        carry = lax.fori_loop(0, loops, two_ticks, (cm1, cm0))
        for t in range(2 + 2 * loops, n_items):
            carry = tick(t, t % 2, carry)
        cm_last, cm_prev = carry
        exponentials((n_items - 1) % 2, cm_last)
        values(n_items - 2, n_items % 2, cm_prev)
        values(n_items - 1, (n_items - 1) % 2, cm_last)

    o_ref[...] = out_scr[SH_QPAD:SH_QPAD + seq, :]


def _dilated_sh(q, k, v):
    b, s, w = q.shape
    max_dil = max(DILATIONS)
    assert DILATIONS[0] == 1 and s % (max_dil * LANES) == 0
    spec = pl.BlockSpec((None, s, LANES), lambda bi, p: (bi, 0, p))
    return pl.pallas_call(
        functools.partial(_dilated_sh_body, seq=s),
        grid=(b, w // LANES),
        in_specs=[pl.BlockSpec((None, s, LANES), lambda bi, p: (bi, 0, p),
                               pipeline_mode=pl.Buffered(1))] * 3,
        out_specs=spec,
        out_shape=jax.ShapeDtypeStruct((b, s, w), BF16),
        scratch_shapes=[pltpu.VMEM((s + 2 * SH_QPAD * max_dil, LANES), BF16),
                        pltpu.VMEM((s + 2 * SH_KPAD * max_dil, LANES), BF16),
                        pltpu.VMEM((LANES, s + 2 * SH_KPAD * max_dil), BF16),
                        pltpu.VMEM((s + 2 * SH_QPAD * max_dil, LANES), F32),
                        pltpu.VMEM((s + 2 * SH_QPAD * max_dil, LANES), F32),
                        pltpu.VMEM((s + 2 * SH_QPAD, LANES), BF16),
                        pltpu.VMEM((2, SH_PAIR, 2, SH_TK, SH_TQ), F32),
                        pltpu.VMEM((2, SH_PAIR, 2, SH_TK, SH_TQ), BF16),
                        pltpu.VMEM((3, SH_TK, SH_TQ), F32)],
        compiler_params=_cparams(("arbitrary", "arbitrary")),
        name="dilated",
    )(q, k, v)


FFT_N2 = 128


FFT_ROWS = SUBLANES


def _fft1_body(zr_ref, zi_ref, f_ref, twc_ref, tws_ref, yr_ref, yi_ref, *, n1):
    f = f_ref[...]
    for j in range(FFT_ROWS):
        zz = jnp.concatenate([zr_ref[:, j, :], zi_ref[:, j, :]], axis=0).astype(BF16)
        y = _dot(f, zz)
        yr, yi = y[:n1], y[n1:]
        cols = slice(j * B_WIDTH, (j + 1) * B_WIDTH)
        c, s = twc_ref[:, cols], tws_ref[:, cols]
        yr_ref[:, j, :] = yr * c + yi * s
        yi_ref[:, j, :] = yi * c - yr * s


def _fft1(zr, zi, fmat, twc, tws):
    b, n1, n2, w = zr.shape
    dat = pl.BlockSpec((None, n1, FFT_ROWS, w), lambda bi, j: (bi, 0, j, 0))
    tw = pl.BlockSpec((n1, FFT_ROWS * w), lambda bi, j: (0, j))
    return pl.pallas_call(
        functools.partial(_fft1_body, n1=n1),
        grid=(b, n2 // FFT_ROWS),
        in_specs=[dat, dat, _resident(fmat.shape), tw, tw],
        out_specs=[dat, dat],
        out_shape=[jax.ShapeDtypeStruct((b, n1, n2, w), F32)] * 2,
        compiler_params=_cparams(("arbitrary", "arbitrary")),
        name="fft_stage1",
    )(zr, zi, fmat, twc, tws)


def _fft2_body(yr_ref, yi_ref, f_ref, o_ref):
    f = f_ref[...]
    for j in range(FFT_ROWS):
        rs = pl.ds(j * FFT_N2, FFT_N2)
        yy = jnp.concatenate([yr_ref[rs, :], yi_ref[rs, :]], axis=0).astype(BF16)
        o_ref[:, j, :] = _dot(f, yy)


def _fft2(yr, yi, fmat):
    b, rows, w = yr.shape
    n1 = rows // FFT_N2
    dat = pl.BlockSpec((None, FFT_ROWS * FFT_N2, w), lambda bi, j: (bi, j, 0))
    return pl.pallas_call(
        _fft2_body,
        grid=(b, n1 // FFT_ROWS),
        in_specs=[dat, dat, _resident(fmat.shape)],
        out_specs=pl.BlockSpec((None, FFT_N2, FFT_ROWS, w), lambda bi, j: (bi, 0, j, 0)),
        out_shape=jax.ShapeDtypeStruct((b, FFT_N2, n1, w), F32),
        compiler_params=_cparams(("arbitrary", "arbitrary")),
        name="fft_stage2",
    )(yr, yi, fmat)


def _odd_in_body(x_ref, g_ref, w_ref, gql_ref, wq_ref, gkvl_ref, wkv_ref, gq_ref, gk_ref,
                 cos_ref, sin_ref, q_ref, k_ref, v_ref, u_ref):
    h = _rms_rows(x_ref[...], g_ref[...]).astype(BF16)
    z = _dot(h, w_ref[...])
    cos, sin = cos_ref[...], sin_ref[...]
    kv_off = C_Q_RANK
    pe_off = kv_off + C_KV_RANK
    rot_off = pe_off + LANES
    u_off = rot_off + LANES
    hw = N_HEADS_C * LANES
    q_lat = _rms_rows(z[:, :C_Q_RANK], gql_ref[...]).astype(BF16)
    kv_lat = _rms_rows(z[:, kv_off:pe_off], gkvl_ref[...]).astype(BF16)
    k_pe = z[:, pe_off:rot_off]
    k_pe_rot = z[:, rot_off:u_off]
    qf = _dot(q_lat, wq_ref[...])
    kvf = _dot(kv_lat, wkv_ref[...])

    def head(t, t_rot, gains, scale):
        ms = jnp.sum(t * t, axis=-1, keepdims=True) * (1.0 / C_QK_DIM)
        r = lax.rsqrt(ms + NORM_EPS) * scale
        return (t * (gains[0:1] * cos) + t_rot * (gains[1:2] * sin)) * r

    gq, gk = gq_ref[...], gk_ref[...]
    for hd in range(N_HEADS_C):
        sl = slice(hd * LANES, (hd + 1) * LANES)
        rot_sl = slice(hw + hd * LANES, hw + (hd + 1) * LANES)
        q_ref[:, sl] = head(qf[:, sl], qf[:, rot_sl], gq,
                            C_QK_DIM ** -0.5 * math.log2(math.e)).astype(BF16)
        k_ref[:, sl] = head(kvf[:, sl] + k_pe, k_pe_rot, gk, 1.0).astype(BF16)
    v_ref[...] = kvf[:, hw:].astype(BF16)
    u_ref[...] = z[:, u_off:]


def _odd_in(x, g, w_in, gql, wq, gkvl, wkv, gq, gk, cos, sin, *, ts):
    b, s, d = x.shape
    tok = lambda w: pl.BlockSpec((None, ts, w), lambda bi, i: (bi, i, 0))
    hw = N_HEADS_C * LANES
    return pl.pallas_call(
        _odd_in_body,
        grid=(b, s // ts),
        in_specs=[tok(d), _resident((1, d)), _resident(w_in.shape), _resident(gql.shape),
                  _resident(wq.shape), _resident(gkvl.shape), _resident(wkv.shape),
                  _resident(gq.shape), _resident(gk.shape), tok(LANES), tok(LANES)],
        out_specs=[tok(hw), tok(hw), tok(N_HEADS_C * C_V_DIM),
                   pl.BlockSpec((ts, D_WIDTH), lambda bi, i: (i, bi))],
        out_shape=[jax.ShapeDtypeStruct((b, s, hw), BF16), jax.ShapeDtypeStruct((b, s, hw), BF16),
                   jax.ShapeDtypeStruct((b, s, N_HEADS_C * C_V_DIM), BF16),
                   jax.ShapeDtypeStruct((s, b * D_WIDTH), F32)],
        compiler_params=_cparams(("arbitrary", "arbitrary")),
        name="odd_in",
    )(x, g, w_in, gql, wq, gkvl, wkv, gq, gk, cos, sin)


FLASH_ONES_ROWS = 16


def _flash_body(q_ref, k_ref, vt_ref, o_ref, s_scr, p_scr, *, tq, tk):
    n_q = q_ref.shape[0] // tq
    n_kv = k_ref.shape[0] // tk
    heads = ((0, 0), (LANES, C_V_DIM))
    ones = jnp.ones((FLASH_ONES_ROWS, tk), BF16)
    items = [(qi, kj) for qi in range(n_q) for kj in range(n_kv)]

    def scores(item, slot):
        qi, kj = item
        col_max = []
        for h, (off, _) in enumerate(heads):
            s = lax.dot_general(k_ref[kj * tk:(kj + 1) * tk, off:off + LANES],
                                q_ref[qi * tq:(qi + 1) * tq, off:off + LANES],
                                (((1,), (1,)), ((), ())), preferred_element_type=F32)
            s_scr[slot, h] = s
            col_max.append(jnp.max(s, axis=0, keepdims=True))
        return col_max

    def softmax(slot, m2, col_max2):
        m_out, alpha_out = [], []
        for h in range(2):
            m_new = jnp.maximum(m2[h], col_max2[h])
            p_scr[slot, h] = jnp.exp2(s_scr[slot, h] - m_new).astype(BF16)
            alpha_out.append(jnp.exp2(m2[h] - m_new))
            m_out.append(m_new)
        return m_out, alpha_out

    def accumulate(item, slot, acc2, alpha2):
        _, kj = item
        out = []
        for h, (_, voff) in enumerate(heads):
            vt = jnp.concatenate([vt_ref[voff:voff + C_V_DIM, kj * tk:(kj + 1) * tk], ones], axis=0)
            out.append(acc2[h] * alpha2[h] + _dot(vt, p_scr[slot, h]))
        return out

    m0 = jnp.full((1, tq), NEG_INF, F32)
    a0 = jnp.zeros((C_V_DIM + FLASH_ONES_ROWS, tq), F32)
    m_state = [[m0, m0] for _ in range(n_q)]
    acc_state = [[a0, a0] for _ in range(n_q)]
    col, alpha = {}, {}
    for t in range(len(items) + 2):
        if t < len(items):
            col[t] = scores(items[t], t % 2)
        if 0 <= t - 1 < len(items):
            qi = items[t - 1][0]
            m_state[qi], alpha[t - 1] = softmax((t - 1) % 2, m_state[qi], col.pop(t - 1))
        if 0 <= t - 2 < len(items):
            qi, kj = items[t - 2]
            acc_state[qi] = accumulate(items[t - 2], t % 2, acc_state[qi], alpha.pop(t - 2))
            if kj == n_kv - 1:
                out_t = jnp.concatenate(
                    [acc[:C_V_DIM] / acc[C_V_DIM:C_V_DIM + 1] for acc in acc_state[qi]], axis=0)
                o_ref[qi * tq:(qi + 1) * tq, :] = out_t.T.astype(o_ref.dtype)


def _flash(q, k, vt, *, tq, tk, n_q):
    b, s, hw = q.shape
    pairs = hw // (2 * LANES)
    bq = tq * n_q
    assert s % bq == 0 and s % tk == 0
    return pl.pallas_call(
        functools.partial(_flash_body, tq=tq, tk=tk),
        grid=(b, pairs, s // bq),
        in_specs=[pl.BlockSpec((None, bq, 2 * LANES), lambda bi, p, i: (bi, i, p)),
                  pl.BlockSpec((None, s, 2 * LANES), lambda bi, p, i: (bi, 0, p)),
                  pl.BlockSpec((None, 2 * C_V_DIM, s), lambda bi, p, i: (bi, p, 0))],
        out_specs=pl.BlockSpec((None, bq, LANES), lambda bi, p, i: (bi, i, p)),
        out_shape=jax.ShapeDtypeStruct((b, s, pairs * LANES), BF16),
        scratch_shapes=[pltpu.VMEM((2, 2, tk, tq), F32), pltpu.VMEM((2, 2, tk, tq), BF16)],
        compiler_params=_cparams(("arbitrary", "arbitrary", "arbitrary")),
        name="flash",
    )(q, k, vt)


def _s5_body(u_ref, bm_ref, cm_ref, lam_ref, y_ref, bu_scr, xs_scr, st_scr, *, tc, nb):
    direction = pl.program_id(0)

    @pl.when(pl.program_id(1) == 0)
    def _():
        st_scr[...] = jnp.zeros(st_scr.shape, F32)

    a_re = jnp.broadcast_to(lam_ref[0:1, :], (nb, S5_LANES))
    a_im = jnp.broadcast_to(lam_ref[1:2, :], (nb, S5_LANES))
    steps_per_block = S5_BLOCK_ROWS // nb
    n_blocks = tc // steps_per_block

    def run(reverse):
        order = range(n_blocks - 1, -1, -1) if reverse else range(n_blocks)
        for blk in order:
            rows = slice(blk * S5_BLOCK_ROWS, (blk + 1) * S5_BLOCK_ROWS)
            bu_scr[rows, :] = _dot(u_ref[rows, :].astype(BF16), bm_ref[...])
        x_re, x_im = st_scr[:, :S5_LANES], st_scr[:, S5_LANES:]
        for blk in order:
            for t in (range(steps_per_block - 1, -1, -1) if reverse else range(steps_per_block)):
                r0 = blk * S5_BLOCK_ROWS + t * nb
                x_re, x_im = (a_re * x_re - a_im * x_im + bu_scr[r0:r0 + nb, :S5_LANES],
                              a_re * x_im + a_im * x_re + bu_scr[r0:r0 + nb, S5_LANES:])
                xs_scr[r0:r0 + nb, :S5_LANES] = x_re
                xs_scr[r0:r0 + nb, S5_LANES:] = x_im
            rows = slice(blk * S5_BLOCK_ROWS, (blk + 1) * S5_BLOCK_ROWS)
            y_ref[rows, :] = _dot(xs_scr[rows, :].astype(BF16), cm_ref[...])
        st_scr[:, :S5_LANES] = x_re
        st_scr[:, S5_LANES:] = x_im

    pl.when(direction == 0)(functools.partial(run, False))
    pl.when(direction == 1)(functools.partial(run, True))


def _s5_scan(u2, bmat, cmat, lam, *, tc, nb):
    rows, dw = u2.shape
    n_chunks = rows // (tc * nb)

    def chunk(d, c):
        return jnp.where(d == 0, c, n_chunks - 1 - c)

    return pl.pallas_call(
        functools.partial(_s5_body, tc=tc, nb=nb),
        grid=(2, n_chunks),
        in_specs=[pl.BlockSpec((tc * nb, dw), lambda d, c: (chunk(d, c), 0)),
                  pl.BlockSpec((None, dw, 2 * S5_LANES), lambda d, c: (d, 0, 0)),
                  pl.BlockSpec((None, 2 * S5_LANES, dw), lambda d, c: (d, 0, 0)),
                  pl.BlockSpec((None, 2, S5_LANES), lambda d, c: (d, 0, 0))],
        out_specs=pl.BlockSpec((None, tc * nb, dw), lambda d, c: (d, chunk(d, c), 0)),
        out_shape=jax.ShapeDtypeStruct((2, rows, dw), F32),
        scratch_shapes=[pltpu.VMEM((tc * nb, 2 * S5_LANES), F32),
                        pltpu.VMEM((tc * nb, 2 * S5_LANES), F32),
                        pltpu.VMEM((nb, 2 * S5_LANES), F32)],
        compiler_params=_cparams(("arbitrary", "arbitrary")),
        name="s5_scan",
    )(u2, bmat, cmat, lam)


def _s5_glu_body(y_ref, u_ref, dskip_ref, w_ref, b_ref, o_ref):
    y = y_ref[0] + y_ref[1] + u_ref[...] * dskip_ref[...]
    z = jax.nn.gelu(y)
    gate = jax.nn.sigmoid(_dot(z.astype(BF16), w_ref[...]) + b_ref[...])
    o_ref[...] = (z * gate).astype(o_ref.dtype)


def _s5_glu(y, u2, dskip, w, bias, *, tr):
    rows, dw = u2.shape
    return pl.pallas_call(
        _s5_glu_body,
        grid=(rows // tr,),
        in_specs=[pl.BlockSpec((2, tr, dw), lambda i: (0, i, 0)),
                  pl.BlockSpec((tr, dw), lambda i: (i, 0)),
                  _resident((1, dw)), _resident(w.shape), _resident((1, dw))],
        out_specs=pl.BlockSpec((tr, dw), lambda i: (i, 0)),
        out_shape=jax.ShapeDtypeStruct((rows, dw), BF16),
        compiler_params=_cparams(("arbitrary",)),
        name="s5_glu",
    )(y, u2, dskip, w, bias)


def _rotary_tables(positions, rot_dim, lane_of_first, period):
    half = rot_dim // 2
    inv_freq = ROPE_THETA ** (-jnp.arange(0, rot_dim, 2, dtype=F32) / rot_dim)
    ang = positions.astype(F32)[..., None] * inv_freq
    ang = jnp.concatenate([ang, ang], axis=-1)
    cos_r, sin_r = jnp.cos(ang), jnp.sin(ang)
    lane = np.arange(LANES)
    j = (lane - lane_of_first) % period
    in_rot = (lane >= lane_of_first) & (j < rot_dim)
    idx = np.where(in_rot, j, 0)
    cos = jnp.where(in_rot, cos_r[..., idx], 1.0)
    sin = jnp.where(in_rot, sin_r[..., idx], 0.0)
    s_up = jnp.where(in_rot & (j < half), -sin, 0.0)
    s_dn = jnp.where(in_rot & (j >= half), sin, 0.0)
    return cos, s_up, s_dn


def _dft_mats(n):
    ang = 2.0 * np.pi * np.outer(np.arange(n), np.arange(n)) / n
    return np.cos(ang), np.sin(ang)


def _block_diag(blocks):
    g, r, c = blocks.shape
    eye = jnp.eye(g, dtype=blocks.dtype)
    return jnp.einsum("grc,gh->grhc", blocks, eye).reshape(g * r, g * c)


def _pick_tile(n, pref):
    t = min(n, pref)
    while n % t:
        t //= 2
    return t


def kernel(x, positions, ffn1_norm, ffn1_w_gate, ffn1_w_up, ffn1_w_down, mix_norm, ffn2_norm, ffn2_w_gate, ffn2_w_up, ffn2_w_down, ab_w_in, ab_w_out, a_q_norm, a_k_norm, b_w_mix, cd_w_in, cd_w_out, c_q_lat_norm, c_w_q_up, c_kv_lat_norm, c_w_kv_up, c_q_norm, c_k_norm, d_lam_re, d_lam_im, d_log_step, d_b_re, d_b_im, d_c_re, d_c_im, d_skip, d_w_glu, d_b_glu):
    bsz, seq, d_model = x.shape
    depth = ffn1_norm.shape[0]
    tokens = bsz * seq
    ts = _pick_tile(seq, 512)
    assert bsz % SUBLANES == 0
    tm = _pick_tile(seq, 512)
    n1 = seq // FFT_N2
    row = lambda v: v.reshape(1, -1).astype(F32)

    cos_a, sup_a, sdn_a = _rotary_tables(positions, ROT_DIM_A, 0, HEAD_DIM)
    cos_c, sup_c, sdn_c = _rotary_tables(positions, C_ROPE_DIM, C_NOPE_DIM, LANES)

    cc, sc_ = _dft_mats(FNET_GROUP_DIM)
    norm = 1.0 / math.sqrt(seq * FNET_GROUP_DIM)
    eye_g = np.eye(FNET_GROUPS)
    dcos = jnp.asarray(np.kron(eye_g, cc) * norm, BF16)
    dsin = jnp.asarray(np.kron(eye_g, -sc_) * norm, BF16)
    c1, s1 = _dft_mats(n1)
    f1 = jnp.asarray(np.block([[c1, s1], [-s1, c1]]), BF16)
    c2, s2 = _dft_mats(FFT_N2)
    f2 = jnp.asarray(np.concatenate([c2, s2], axis=1), BF16)
    tw_ang = 2.0 * np.pi * np.outer(np.arange(n1), np.arange(FFT_N2)) / seq
    twc = jnp.repeat(jnp.asarray(np.cos(tw_ang), F32), B_WIDTH, axis=1)
    tws = jnp.repeat(jnp.asarray(np.sin(tw_ang), F32), B_WIDTH, axis=1)
    head_mean = jnp.asarray(np.kron(np.eye(N_HEADS_A), np.full((HEAD_DIM, HEAD_DIM), 1.0 / HEAD_DIM)), BF16)

    x2 = x.reshape(tokens, d_model)
    for layer in range(depth):
        i = layer // 2
        x2 = _ffn(x2, row(ffn1_norm[layer]), ffn1_w_gate[layer].astype(BF16),
                  ffn1_w_up[layer].astype(BF16), ffn1_w_down[layer].astype(BF16), tm=tm)
        x3 = x2.reshape(bsz, seq, d_model)
        if layer % 2 == 0:
            mbd = _block_diag(b_w_mix[i]).astype(BF16)
            q, k, v, zr, zi = _even_in(
                x3, row(mix_norm[layer]), ab_w_in[i].astype(BF16), head_mean,
                row(jnp.tile(a_q_norm[i], N_HEADS_A)), row(jnp.tile(a_k_norm[i], N_HEADS_A)),
                cos_a, sup_a, sdn_a, dcos, dsin, mbd, ts=ts)
            a_out = _dilated_sh(q, k, v)
            shp = (bsz, n1, FFT_N2, B_WIDTH)
            yr, yi = _fft1(zr.reshape(shp), zi.reshape(shp), f1, twc, tws)
            shp2 = (bsz, n1 * FFT_N2, B_WIDTH)
            spec = _fft2(yr.reshape(shp2), yi.reshape(shp2), f2)
            w_out = ab_w_out[i].astype(BF16)
            mixer = (a_out.reshape(tokens, A_WIDTH), spec.reshape(tokens, B_WIDTH),
                     w_out[:A_WIDTH], w_out[A_WIDTH:],
                     pl.BlockSpec((tm, B_WIDTH), lambda r: (r, 0)))
        else:
            half = C_ROPE_DIM // 2
            rot_cols = lambda w: jnp.concatenate([-w[..., half:], w[..., :half]], axis=-1)
            swap_halves = lambda w: jnp.concatenate([w[..., half:], w[..., :half]], axis=-1)
            rope_slab = lambda w: jnp.pad(
                w, [(0, 0)] * (w.ndim - 1) + [(C_NOPE_DIM, LANES - C_QK_DIM)])
            w_in = cd_w_in[i]
            pe_off = C_Q_RANK + C_KV_RANK
            w_pe = w_in[:, pe_off:pe_off + C_ROPE_DIM]
            w_in_p = jnp.concatenate([w_in[:, :pe_off], rope_slab(w_pe), rope_slab(rot_cols(w_pe)),
                                      w_in[:, pe_off + C_ROPE_DIM:]], axis=1).astype(BF16)
            pad_h = lambda w, keep: jnp.pad(
                w, ((0, 0), (0, 0), (0, LANES - keep))).reshape(w.shape[0], N_HEADS_C * LANES)
            wq = c_w_q_up[i].reshape(C_Q_RANK, N_HEADS_C, C_QK_DIM)
            wq_p = jnp.concatenate(
                [pad_h(wq, C_QK_DIM),
                 rope_slab(rot_cols(wq[..., C_NOPE_DIM:])).reshape(C_Q_RANK, N_HEADS_C * LANES)],
                axis=1).astype(BF16)
            wkv = c_w_kv_up[i].reshape(C_KV_RANK, N_HEADS_C, C_NOPE_DIM + C_V_DIM)
            wkv_p = jnp.concatenate(
                [pad_h(wkv[:, :, :C_NOPE_DIM], C_NOPE_DIM),
                 wkv[:, :, C_NOPE_DIM:].reshape(C_KV_RANK, N_HEADS_C * C_V_DIM)], axis=1).astype(BF16)
            gains = lambda gvec: jnp.stack(
                [jnp.pad(gvec, (0, LANES - C_QK_DIM)),
                 rope_slab(swap_halves(gvec[C_NOPE_DIM:]))]).astype(F32)
            q, k, v, u_t = _odd_in(
                x3, row(mix_norm[layer]), w_in_p, row(c_q_lat_norm[i]), wq_p,
                row(c_kv_lat_norm[i]), wkv_p, gains(c_q_norm[i]), gains(c_k_norm[i]),
                cos_c, sdn_c - sup_c, ts=ts)
            c_out = _flash(q, k, jnp.swapaxes(v, 1, 2), tq=_pick_tile(seq, 256),
                           tk=_pick_tile(seq, 256), n_q=2)

            lam = lax.complex(d_lam_re[i].astype(F32), d_lam_im[i].astype(F32))
            step = jnp.exp(d_log_step[i].astype(F32))[..., None]
            lam_bar = jnp.exp(lam * step)
            b_bar = ((lam_bar - 1.0) / lam)[..., None] * lax.complex(
                d_b_re[i].astype(F32), d_b_im[i].astype(F32))
            to_bd = lambda m: jnp.stack([_block_diag(m[d]) for d in range(2)])
            b_t = jnp.swapaxes(b_bar, -1, -2)
            bmat = jnp.concatenate([to_bd(jnp.real(b_t)), to_bd(jnp.imag(b_t))], axis=-1).astype(BF16)
            c_re_t = jnp.swapaxes(d_c_re[i].astype(F32), -1, -2)
            c_im_t = jnp.swapaxes(d_c_im[i].astype(F32), -1, -2)
            cmat = jnp.concatenate([to_bd(c_re_t), to_bd(-c_im_t)], axis=-2).astype(BF16)
            lam2 = jnp.stack([jnp.real(lam_bar).reshape(2, S5_LANES),
                              jnp.imag(lam_bar).reshape(2, S5_LANES)], axis=1)

            u2 = u_t.reshape(seq * bsz, D_WIDTH)
            tc = _pick_tile(seq, 128)
            y = _s5_scan(u2, bmat, cmat, lam2, tc=tc, nb=bsz)
            d2 = _s5_glu(y, u2, row(d_skip[i]), d_w_glu[i].astype(BF16), row(d_b_glu[i]),
                         tr=_pick_tile(seq * bsz, 1024))
            d_out = d2.reshape(seq, bsz * D_WIDTH)
            w_out = cd_w_out[i].astype(BF16)
            tiles_per_seq = seq // tm
            mixer = (c_out.reshape(tokens, N_HEADS_C * C_V_DIM), d_out,
                     w_out[:N_HEADS_C * C_V_DIM], w_out[N_HEADS_C * C_V_DIM:],
                     pl.BlockSpec((tm, D_WIDTH),
                                  lambda r: (r % tiles_per_seq, r // tiles_per_seq)))
        x2 = _ffn(x2, row(ffn2_norm[layer]), ffn2_w_gate[layer].astype(BF16),
                  ffn2_w_up[layer].astype(BF16), ffn2_w_down[layer].astype(BF16), tm=tm,
                  mixer=mixer)
    return x2.reshape(bsz, seq, d_model)
```

```python
import functools
import math

import jax
import jax.numpy as jnp
import numpy as np
from jax import lax
from jax.experimental import pallas as pl
from jax.experimental.pallas import tpu as pltpu

F32 = jnp.float32
BF16 = jnp.bfloat16

HEAD_DIM = 64
N_HEADS_A = 8
ROT_DIM_A = 16
DILATIONS = (1, 4, 16)
RADIUS = 64
FNET_GROUPS = 4
FNET_GROUP_DIM = 64
N_HEADS_C = 8
C_NOPE_DIM = 64
C_ROPE_DIM = 32
C_V_DIM = 64
C_QK_DIM = C_NOPE_DIM + C_ROPE_DIM
C_Q_RANK = 256
C_KV_RANK = 128
S5_GROUPS = 16
S5_GROUP_DIM = 16
S5_STATE = 64
ROPE_THETA = 500000.0
NORM_EPS = 1e-6
NEG_INF = -1e30
A_WIDTH = N_HEADS_A * HEAD_DIM
B_WIDTH = FNET_GROUPS * FNET_GROUP_DIM
D_WIDTH = S5_GROUPS * S5_GROUP_DIM
S5_LANES = S5_GROUPS * S5_STATE
S5_BLOCK_ROWS = 128

LANES = 128
SUBLANES = 8
VMEM_LIMIT_BYTES = 56 * 1024 * 1024


def _cparams(sem):
    return pltpu.CompilerParams(dimension_semantics=sem, vmem_limit_bytes=VMEM_LIMIT_BYTES)


def _resident(shape):
    nd = len(shape)
    return pl.BlockSpec(shape, lambda *_: (0,) * nd, pipeline_mode=pl.Buffered(1))


def _rms_rows(x, g):
    ms = jnp.mean(x * x, axis=-1, keepdims=True)
    return x * lax.rsqrt(ms + NORM_EPS) * g


def _dot(a, b):
    return jnp.dot(a, b, preferred_element_type=F32)


FFN_CHUNK = 256


def _ffn_body(*refs, n_chunks, fused_mixer):
    if fused_mixer:
        x_ref, a_ref, b_ref, wa_ref, wb_ref, g_ref, wg_ref, wu_ref, wd_ref, o_ref = refs
        x = (x_ref[...] + _dot(a_ref[...], wa_ref[...])
             + _dot(b_ref[...].astype(BF16), wb_ref[...]))
    else:
        x_ref, g_ref, wg_ref, wu_ref, wd_ref, o_ref = refs
        x = x_ref[...]
    h = _rms_rows(x, g_ref[...]).astype(BF16)
    acc = jnp.zeros(x.shape, F32)
    for c in range(n_chunks):
        sl = pl.ds(c * FFN_CHUNK, FFN_CHUNK)
        gate = _dot(h, wg_ref[:, sl])
        up = _dot(h, wu_ref[:, sl])
        act = (gate * jax.nn.sigmoid(gate) * up).astype(BF16)
        acc = acc + _dot(act, wd_ref[sl, :])
    o_ref[...] = x + 0.5 * acc


def _ffn(x2, g, wg, wu, wd, *, tm, mixer=None):
    t, d = x2.shape
    f = wg.shape[1]
    assert t % tm == 0 and f % FFN_CHUNK == 0
    rows = lambda w: pl.BlockSpec((tm, w), lambda i: (i, 0))
    weights = [_resident((1, d)), _resident((d, f)), _resident((d, f)), _resident((f, d))]
    if mixer is None:
        in_specs, args = [rows(d)] + weights, (x2, g, wg, wu, wd)
    else:
        a, b, wa, wb, b_spec = mixer
        in_specs = [rows(d), rows(a.shape[1]), b_spec, _resident(wa.shape), _resident(wb.shape)] + weights
        args = (x2, a, b, wa, wb, g, wg, wu, wd)
    return pl.pallas_call(
        functools.partial(_ffn_body, n_chunks=f // FFN_CHUNK, fused_mixer=mixer is not None),
        grid=(t // tm,),
        in_specs=in_specs,
        out_specs=rows(d),
        out_shape=jax.ShapeDtypeStruct((t, d), F32),
        compiler_params=_cparams(("arbitrary",)),
        name="ffn_mix" if mixer is not None else "ffn",
    )(*args)


def _rotary_block(t, cos, s_up, s_dn, shift):
    return (t * cos + pltpu.roll(t, LANES - shift, axis=1) * s_up
            + pltpu.roll(t, shift, axis=1) * s_dn)


def _even_in_body(x_ref, g_ref, w_ref, pm_ref, gq_ref, gk_ref, cos_ref, sup_ref, sdn_ref,
                  dcos_ref, dsin_ref, mbd_ref,
                  q_ref, k_ref, v_ref, zr_ref, zi_ref, wc_scr):
    first = jnp.logical_and(pl.program_id(0) == 0, pl.program_id(1) == 0)

    @pl.when(first)
    def _():
        mbd = mbd_ref[...]
        wc_scr[:, :B_WIDTH] = _dot(dcos_ref[...], mbd).astype(BF16)
        wc_scr[:, B_WIDTH:] = _dot(dsin_ref[...], mbd).astype(BF16)

    h = _rms_rows(x_ref[...], g_ref[...]).astype(BF16)
    z = _dot(h, w_ref[...])
    cos, s_up, s_dn = cos_ref[...], sup_ref[...], sdn_ref[...]
    pm = pm_ref[...]

    def head_norm_rot(t, gain, scale):
        ms = _dot((t * t).astype(BF16), pm)
        tn = t * lax.rsqrt(ms + NORM_EPS) * gain
        blocks = [_rotary_block(tn[:, c * LANES:(c + 1) * LANES], cos, s_up, s_dn, ROT_DIM_A // 2)
                  for c in range(A_WIDTH // LANES)]
        out = jnp.concatenate(blocks, axis=1)
        return out * scale if scale != 1.0 else out

    q_ref[...] = head_norm_rot(z[:, :A_WIDTH], gq_ref[...], HEAD_DIM ** -0.5 * math.log2(math.e))
    k_ref[...] = head_norm_rot(z[:, A_WIDTH:2 * A_WIDTH], gk_ref[...], 1.0)
    v_ref[...] = z[:, 2 * A_WIDTH:3 * A_WIDTH]
    zz = _dot(z[:, 3 * A_WIDTH:].astype(BF16), wc_scr[...])
    zr_ref[...] = zz[:, :B_WIDTH]
    zi_ref[...] = zz[:, B_WIDTH:]


def _even_in(x, g, w_in, pm, gq, gk, cos, s_up, s_dn, dcos, dsin, mbd, *, ts):
    b, s, d = x.shape
    tok = lambda w: pl.BlockSpec((None, ts, w), lambda bi, i: (bi, i, 0))
    return pl.pallas_call(
        _even_in_body,
        grid=(b, s // ts),
        in_specs=[tok(d), _resident((1, d)), _resident(w_in.shape), _resident(pm.shape),
                  _resident((1, A_WIDTH)), _resident((1, A_WIDTH)),
                  tok(LANES), tok(LANES), tok(LANES),
                  _resident(dcos.shape), _resident(dsin.shape), _resident(mbd.shape)],
        out_specs=[tok(A_WIDTH), tok(A_WIDTH), tok(A_WIDTH), tok(B_WIDTH), tok(B_WIDTH)],
        out_shape=[jax.ShapeDtypeStruct((b, s, A_WIDTH), F32)] * 3
        + [jax.ShapeDtypeStruct((b, s, B_WIDTH), F32)] * 2,
        scratch_shapes=[pltpu.VMEM((B_WIDTH, 2 * B_WIDTH), BF16)],
        compiler_params=_cparams(("arbitrary", "arbitrary")),
        name="even_in",
    )(x, g, w_in, pm, gq, gk, cos, s_up, s_dn, dcos, dsin, mbd)


DIL_ONES_ROWS = 16
SH_TQ = LANES
SH_TK = 2 * LANES
SH_QPAD = RADIUS
SH_KPAD = LANES
SH_PAIR = 4


def _dilated_sh_body(q_ref, k_ref, v_ref, o_ref, qd, kd, vt, o_st, l_st, out_scr, s_scr, p_scr,
                     bias_scr, *, seq):
    lane = lax.broadcasted_iota(jnp.int32, (1, LANES), 1)
    head_lanes = (lane < HEAD_DIM, lane >= HEAD_DIM)
    kk = lax.broadcasted_iota(jnp.int32, (SH_TK, SH_TQ), 0)
    qq = lax.broadcasted_iota(jnp.int32, (SH_TK, SH_TQ), 1)
    band = jnp.abs(kk - qq - (SH_KPAD - SH_QPAD)) <= RADIUS
    key_ok = (kk >= SH_KPAD, kk >= 0, kk < SH_KPAD)
    for idx in range(3):
        bias_scr[idx] = jnp.where(jnp.logical_and(band, key_ok[idx]), 0.0, NEG_INF)
    ones = jnp.ones((DIL_ONES_ROWS, SH_TK), BF16)
    max_dil = max(DILATIONS)
    state_pad = SH_QPAD * max_dil

    order = tuple(reversed(DILATIONS))
    for dil in order:
        first, last = dil == order[0], dil == order[-1]
        sub_len = seq // dil
        chunks = sub_len // LANES
        tiles_per_class = chunks + 1
        n_tiles = dil * tiles_per_class
        n_items = -(-n_tiles // SH_PAIR)
        assert last or n_tiles % SH_PAIR == 0
        q_stride = sub_len + 2 * SH_QPAD
        k_stride = sub_len + 2 * SH_KPAD

        def zero_pads(res, carry, sub_len=sub_len, q_stride=q_stride, k_stride=k_stride):
            zq = jnp.zeros((SH_QPAD, LANES), BF16)
            zk = jnp.zeros((SH_KPAD, LANES), BF16)
            for off in (0, SH_QPAD + sub_len):
                qd[pl.ds(pl.multiple_of(res * q_stride + off, SH_QPAD), SH_QPAD), :] = zq
            for off in (0, SH_KPAD + sub_len):
                pad = pl.ds(pl.multiple_of(res * k_stride + off, SH_KPAD), SH_KPAD)
                kd[pad, :] = zk
                vt[:, pad] = jnp.zeros((LANES, SH_KPAD), BF16)
            return carry

        lax.fori_loop(0, dil, zero_pads, 0)

        def prepare(idx, carry, dil=dil, chunks=chunks, q_stride=q_stride, k_stride=k_stride):
            res = idx // chunks
            c0 = (idx - res * chunks) * LANES
            if dil == 1:
                src = pl.ds(pl.multiple_of(c0, LANES), LANES)
            else:
                src = pl.ds(res + dil * c0, LANES, stride=dil)
            k_dst = pl.ds(pl.multiple_of(res * k_stride + SH_KPAD + c0, LANES), LANES)
            qd[pl.ds(pl.multiple_of(res * q_stride + SH_QPAD + c0, SH_QPAD), LANES), :] = (
                q_ref[src, :].astype(BF16))
            kd[k_dst, :] = k_ref[src, :].astype(BF16)
            vt[:, k_dst] = v_ref[src, :].T.astype(BF16)
            return carry

        lax.fori_loop(0, seq // LANES, prepare, 0, unroll=4)

        def geometry(tile, tiles_per_class=tiles_per_class, n_tiles=n_tiles):
            tile = jnp.minimum(tile, n_tiles - 1)
            res = tile // tiles_per_class
            i = tile - res * tiles_per_class
            edge = (i > 0).astype(jnp.int32) + (i == tiles_per_class - 1).astype(jnp.int32)
            return res, i, edge

        def scores(item, slot, geometry=geometry, q_stride=q_stride, k_stride=k_stride):
            col_max = []
            for u in range(SH_PAIR):
                res, i, edge = geometry(item * SH_PAIR + u)
                k_win = kd[pl.ds(pl.multiple_of(res * k_stride + i * LANES, LANES), SH_TK), :]
                q2 = qd[pl.ds(pl.multiple_of(res * q_stride + i * LANES, LANES), SH_TQ), :]
                bias = bias_scr[edge]
                for h in range(2):
                    qh = jnp.where(head_lanes[h], q2, jnp.zeros_like(q2))
                    s = lax.dot_general(k_win, qh, (((1,), (1,)), ((), ())),
                                        preferred_element_type=F32) + bias
                    s_scr[slot, u, h] = s
                    col_max.append(jnp.max(s, axis=0, keepdims=True))
            return tuple(col_max)

        def exponentials(slot, col_max):
            for u in range(SH_PAIR):
                for h in range(2):
                    p_scr[slot, u, h] = jnp.exp2(s_scr[slot, u, h] - col_max[2 * u + h]).astype(BF16)

        def values(item, slot, col_max, geometry=geometry, dil=dil, k_stride=k_stride,
                   first=first, last=last):
            for u in range(SH_PAIR):
                res, i, _ = geometry(item * SH_PAIR + u)
                cols = pl.ds(pl.multiple_of(res * k_stride + i * LANES, LANES), SH_TK)
                o_t, l_t = [], []
                for h in range(2):
                    vth = jnp.concatenate([vt[h * HEAD_DIM:(h + 1) * HEAD_DIM, cols], ones], axis=0)
                    acc = _dot(vth, p_scr[slot, u, h])
                    denom = acc[HEAD_DIM:HEAD_DIM + 1]
                    o_t.append(acc[:HEAD_DIM] / denom)
                    l_t.append(jnp.broadcast_to(col_max[2 * u + h] + jnp.log2(denom),
                                                (HEAD_DIM, SH_TQ)))
                o_new = jnp.concatenate(o_t, axis=0).T
                l_new = jnp.concatenate(l_t, axis=0).T
                start = state_pad + res + dil * (i * LANES - SH_QPAD)
                if dil == 1:
                    st_rows = pl.ds(pl.multiple_of(start, SH_QPAD), SH_TQ)
                else:
                    st_rows = pl.ds(start, SH_TQ, stride=dil)
                if not first:
                    o_old, l_old = o_st[st_rows, :], l_st[st_rows, :]
                    l_max = jnp.maximum(l_old, l_new)
                    w_old, w_new = jnp.exp2(l_old - l_max), jnp.exp2(l_new - l_max)
                    w_sum = w_old + w_new
                    o_new = (w_old * o_old + w_new * o_new) / w_sum
                    l_new = l_max + jnp.log2(w_sum)
                if last:
                    out_scr[pl.ds(pl.multiple_of(i * LANES, LANES), SH_TQ), :] = o_new.astype(BF16)
                else:
                    o_st[st_rows, :] = o_new
                    l_st[st_rows, :] = l_new

        def tick(t, parity, carry, scores=scores, exponentials=exponentials, values=values):
            cm_prev, cm_prev2 = carry
            cm_new = scores(t, parity)
            exponentials(1 - parity, cm_prev)
            values(t - 2, parity, cm_prev2)
            return cm_new, cm_prev

        cm0 = scores(0, 0)
        cm1 = scores(1, 1)
        exponentials(0, cm0)

        def two_ticks(j, carry, tick=tick):
            t = 2 * j + 2
            return tick(t + 1, 1, tick(t, 0, carry))

        loops = (n_items - 2) // 2
        carry = lax.fori_loop(0, loops, two_ticks, (cm1, cm0))
        for t in range(2 + 2 * loops, n_items):
            carry = tick(t, t % 2, carry)
        cm_last, cm_prev = carry
        exponentials((n_items - 1) % 2, cm_last)
        values(n_items - 2, n_items % 2, cm_prev)
        values(n_items - 1, (n_items - 1) % 2, cm_last)

    o_ref[...] = out_scr[SH_QPAD:SH_QPAD + seq, :]


def _dilated_sh(q, k, v):
    b, s, w = q.shape
    max_dil = max(DILATIONS)
    assert DILATIONS[0] == 1 and s % (max_dil * LANES) == 0
    spec = pl.BlockSpec((None, s, LANES), lambda bi, p: (bi, 0, p))
    return pl.pallas_call(
        functools.partial(_dilated_sh_body, seq=s),
        grid=(b, w // LANES),
        in_specs=[pl.BlockSpec((None, s, LANES), lambda bi, p: (bi, 0, p),
                               pipeline_mode=pl.Buffered(1))] * 3,
        out_specs=spec,
        out_shape=jax.ShapeDtypeStruct((b, s, w), BF16),
        scratch_shapes=[pltpu.VMEM((s + 2 * SH_QPAD * max_dil, LANES), BF16),
                        pltpu.VMEM((s + 2 * SH_KPAD * max_dil, LANES), BF16),
                        pltpu.VMEM((LANES, s + 2 * SH_KPAD * max_dil), BF16),
                        pltpu.VMEM((s + 2 * SH_QPAD * max_dil, LANES), F32),
                        pltpu.VMEM((s + 2 * SH_QPAD * max_dil, LANES), F32),
                        pltpu.VMEM((s + 2 * SH_QPAD, LANES), BF16),
                        pltpu.VMEM((2, SH_PAIR, 2, SH_TK, SH_TQ), F32),
                        pltpu.VMEM((2, SH_PAIR, 2, SH_TK, SH_TQ), BF16),
                        pltpu.VMEM((3, SH_TK, SH_TQ), F32)],
        compiler_params=_cparams(("arbitrary", "arbitrary")),
        name="dilated",
    )(q, k, v)


FFT_N2 = 128


FFT_ROWS = SUBLANES


def _fft1_body(zr_ref, zi_ref, f_ref, twc_ref, tws_ref, yr_ref, yi_ref, *, n1):
    f = f_ref[...]
    for j in range(FFT_ROWS):
        zz = jnp.concatenate([zr_ref[:, j, :], zi_ref[:, j, :]], axis=0).astype(BF16)
        y = _dot(f, zz)
        yr, yi = y[:n1], y[n1:]
        cols = slice(j * B_WIDTH, (j + 1) * B_WIDTH)
        c, s = twc_ref[:, cols], tws_ref[:, cols]
        yr_ref[:, j, :] = yr * c + yi * s
        yi_ref[:, j, :] = yi * c - yr * s


def _fft1(zr, zi, fmat, twc, tws):
    b, n1, n2, w = zr.shape
    dat = pl.BlockSpec((None, n1, FFT_ROWS, w), lambda bi, j: (bi, 0, j, 0))
    tw = pl.BlockSpec((n1, FFT_ROWS * w), lambda bi, j: (0, j))
    return pl.pallas_call(
        functools.partial(_fft1_body, n1=n1),
        grid=(b, n2 // FFT_ROWS),
        in_specs=[dat, dat, _resident(fmat.shape), tw, tw],
        out_specs=[dat, dat],
        out_shape=[jax.ShapeDtypeStruct((b, n1, n2, w), F32)] * 2,
        compiler_params=_cparams(("arbitrary", "arbitrary")),
        name="fft_stage1",
    )(zr, zi, fmat, twc, tws)


def _fft2_body(yr_ref, yi_ref, f_ref, o_ref):
    f = f_ref[...]
    for j in range(FFT_ROWS):
        rs = pl.ds(j * FFT_N2, FFT_N2)
        yy = jnp.concatenate([yr_ref[rs, :], yi_ref[rs, :]], axis=0).astype(BF16)
        o_ref[:, j, :] = _dot(f, yy)


def _fft2(yr, yi, fmat):
    b, rows, w = yr.shape
    n1 = rows // FFT_N2
    dat = pl.BlockSpec((None, FFT_ROWS * FFT_N2, w), lambda bi, j: (bi, j, 0))
    return pl.pallas_call(
        _fft2_body,
        grid=(b, n1 // FFT_ROWS),
        in_specs=[dat, dat, _resident(fmat.shape)],
        out_specs=pl.BlockSpec((None, FFT_N2, FFT_ROWS, w), lambda bi, j: (bi, 0, j, 0)),
        out_shape=jax.ShapeDtypeStruct((b, FFT_N2, n1, w), F32),
        compiler_params=_cparams(("arbitrary", "arbitrary")),
        name="fft_stage2",
    )(yr, yi, fmat)


def _odd_in_body(x_ref, g_ref, w_ref, gql_ref, wq_ref, gkvl_ref, wkv_ref, gq_ref, gk_ref,
                 cos_ref, sin_ref, q_ref, k_ref, v_ref, u_ref):
    h = _rms_rows(x_ref[...], g_ref[...]).astype(BF16)
    z = _dot(h, w_ref[...])
    cos, sin = cos_ref[...], sin_ref[...]
    kv_off = C_Q_RANK
    pe_off = kv_off + C_KV_RANK
    rot_off = pe_off + LANES
    u_off = rot_off + LANES
    hw = N_HEADS_C * LANES
    q_lat = _rms_rows(z[:, :C_Q_RANK], gql_ref[...]).astype(BF16)
    kv_lat = _rms_rows(z[:, kv_off:pe_off], gkvl_ref[...]).astype(BF16)
    k_pe = z[:, pe_off:rot_off]
    k_pe_rot = z[:, rot_off:u_off]
    qf = _dot(q_lat, wq_ref[...])
    kvf = _dot(kv_lat, wkv_ref[...])

    def head(t, t_rot, gains, scale):
        ms = jnp.sum(t * t, axis=-1, keepdims=True) * (1.0 / C_QK_DIM)
        r = lax.rsqrt(ms + NORM_EPS) * scale
        return (t * (gains[0:1] * cos) + t_rot * (gains[1:2] * sin)) * r

    gq, gk = gq_ref[...], gk_ref[...]
    for hd in range(N_HEADS_C):
        sl = slice(hd * LANES, (hd + 1) * LANES)
        rot_sl = slice(hw + hd * LANES, hw + (hd + 1) * LANES)
        q_ref[:, sl] = head(qf[:, sl], qf[:, rot_sl], gq,
                            C_QK_DIM ** -0.5 * math.log2(math.e)).astype(BF16)
        k_ref[:, sl] = head(kvf[:, sl] + k_pe, k_pe_rot, gk, 1.0).astype(BF16)
    v_ref[...] = kvf[:, hw:].astype(BF16)
    u_ref[...] = z[:, u_off:]


def _odd_in(x, g, w_in, gql, wq, gkvl, wkv, gq, gk, cos, sin, *, ts):
    b, s, d = x.shape
    tok = lambda w: pl.BlockSpec((None, ts, w), lambda bi, i: (bi, i, 0))
    hw = N_HEADS_C * LANES
    return pl.pallas_call(
        _odd_in_body,
        grid=(b, s // ts),
        in_specs=[tok(d), _resident((1, d)), _resident(w_in.shape), _resident(gql.shape),
                  _resident(wq.shape), _resident(gkvl.shape), _resident(wkv.shape),
                  _resident(gq.shape), _resident(gk.shape), tok(LANES), tok(LANES)],
        out_specs=[tok(hw), tok(hw), tok(N_HEADS_C * C_V_DIM),
                   pl.BlockSpec((ts, D_WIDTH), lambda bi, i: (i, bi))],
        out_shape=[jax.ShapeDtypeStruct((b, s, hw), BF16), jax.ShapeDtypeStruct((b, s, hw), BF16),
                   jax.ShapeDtypeStruct((b, s, N_HEADS_C * C_V_DIM), BF16),
                   jax.ShapeDtypeStruct((s, b * D_WIDTH), F32)],
        compiler_params=_cparams(("arbitrary", "arbitrary")),
        name="odd_in",
    )(x, g, w_in, gql, wq, gkvl, wkv, gq, gk, cos, sin)


FLASH_ONES_ROWS = 16


def _flash_body(q_ref, k_ref, vt_ref, o_ref, s_scr, p_scr, *, tq, tk):
    n_q = q_ref.shape[0] // tq
    n_kv = k_ref.shape[0] // tk
    heads = ((0, 0), (LANES, C_V_DIM))
    ones = jnp.ones((FLASH_ONES_ROWS, tk), BF16)
    items = [(qi, kj) for qi in range(n_q) for kj in range(n_kv)]

    def scores(item, slot):
        qi, kj = item
        col_max = []
        for h, (off, _) in enumerate(heads):
            s = lax.dot_general(k_ref[kj * tk:(kj + 1) * tk, off:off + LANES],
                                q_ref[qi * tq:(qi + 1) * tq, off:off + LANES],
                                (((1,), (1,)), ((), ())), preferred_element_type=F32)
            s_scr[slot, h] = s
            col_max.append(jnp.max(s, axis=0, keepdims=True))
        return col_max

    def softmax(slot, m2, col_max2):
        m_out, alpha_out = [], []
        for h in range(2):
            m_new = jnp.maximum(m2[h], col_max2[h])
            p_scr[slot, h] = jnp.exp2(s_scr[slot, h] - m_new).astype(BF16)
            alpha_out.append(jnp.exp2(m2[h] - m_new))
            m_out.append(m_new)
        return m_out, alpha_out

    def accumulate(item, slot, acc2, alpha2):
        _, kj = item
        out = []
        for h, (_, voff) in enumerate(heads):
            vt = jnp.concatenate([vt_ref[voff:voff + C_V_DIM, kj * tk:(kj + 1) * tk], ones], axis=0)
            out.append(acc2[h] * alpha2[h] + _dot(vt, p_scr[slot, h]))
        return out

    m0 = jnp.full((1, tq), NEG_INF, F32)
    a0 = jnp.zeros((C_V_DIM + FLASH_ONES_ROWS, tq), F32)
    m_state = [[m0, m0] for _ in range(n_q)]
    acc_state = [[a0, a0] for _ in range(n_q)]
    col, alpha = {}, {}
    for t in range(len(items) + 2):
        if t < len(items):
            col[t] = scores(items[t], t % 2)
        if 0 <= t - 1 < len(items):
            qi = items[t - 1][0]
            m_state[qi], alpha[t - 1] = softmax((t - 1) % 2, m_state[qi], col.pop(t - 1))
        if 0 <= t - 2 < len(items):
            qi, kj = items[t - 2]
            acc_state[qi] = accumulate(items[t - 2], t % 2, acc_state[qi], alpha.pop(t - 2))
            if kj == n_kv - 1:
                out_t = jnp.concatenate(
                    [acc[:C_V_DIM] / acc[C_V_DIM:C_V_DIM + 1] for acc in acc_state[qi]], axis=0)
                o_ref[qi * tq:(qi + 1) * tq, :] = out_t.T.astype(o_ref.dtype)


def _flash(q, k, vt, *, tq, tk, n_q):
    b, s, hw = q.shape
    pairs = hw // (2 * LANES)
    bq = tq * n_q
    assert s % bq == 0 and s % tk == 0
    return pl.pallas_call(
        functools.partial(_flash_body, tq=tq, tk=tk),
        grid=(b, pairs, s // bq),
        in_specs=[pl.BlockSpec((None, bq, 2 * LANES), lambda bi, p, i: (bi, i, p)),
                  pl.BlockSpec((None, s, 2 * LANES), lambda bi, p, i: (bi, 0, p)),
                  pl.BlockSpec((None, 2 * C_V_DIM, s), lambda bi, p, i: (bi, p, 0))],
        out_specs=pl.BlockSpec((None, bq, LANES), lambda bi, p, i: (bi, i, p)),
        out_shape=jax.ShapeDtypeStruct((b, s, pairs * LANES), BF16),
        scratch_shapes=[pltpu.VMEM((2, 2, tk, tq), F32), pltpu.VMEM((2, 2, tk, tq), BF16)],
        compiler_params=_cparams(("arbitrary", "arbitrary", "arbitrary")),
        name="flash",
    )(q, k, vt)


def _s5_body(u_ref, bm_ref, cm_ref, lam_ref, y_ref, bu_scr, xs_scr, st_scr, *, tc, nb):
    direction = pl.program_id(0)

    @pl.when(pl.program_id(1) == 0)
    def _():
        st_scr[...] = jnp.zeros(st_scr.shape, F32)

    a_re = jnp.broadcast_to(lam_ref[0:1, :], (nb, S5_LANES))
    a_im = jnp.broadcast_to(lam_ref[1:2, :], (nb, S5_LANES))
    steps_per_block = S5_BLOCK_ROWS // nb
    n_blocks = tc // steps_per_block

    def run(reverse):
        order = range(n_blocks - 1, -1, -1) if reverse else range(n_blocks)
        for blk in order:
            rows = slice(blk * S5_BLOCK_ROWS, (blk + 1) * S5_BLOCK_ROWS)
            bu_scr[rows, :] = _dot(u_ref[rows, :].astype(BF16), bm_ref[...])
        x_re, x_im = st_scr[:, :S5_LANES], st_scr[:, S5_LANES:]
        for blk in order:
            for t in (range(steps_per_block - 1, -1, -1) if reverse else range(steps_per_block)):
                r0 = blk * S5_BLOCK_ROWS + t * nb
                x_re, x_im = (a_re * x_re - a_im * x_im + bu_scr[r0:r0 + nb, :S5_LANES],
                              a_re * x_im + a_im * x_re + bu_scr[r0:r0 + nb, S5_LANES:])
                xs_scr[r0:r0 + nb, :S5_LANES] = x_re
                xs_scr[r0:r0 + nb, S5_LANES:] = x_im
            rows = slice(blk * S5_BLOCK_ROWS, (blk + 1) * S5_BLOCK_ROWS)
            y_ref[rows, :] = _dot(xs_scr[rows, :].astype(BF16), cm_ref[...])
        st_scr[:, :S5_LANES] = x_re
        st_scr[:, S5_LANES:] = x_im

    pl.when(direction == 0)(functools.partial(run, False))
    pl.when(direction == 1)(functools.partial(run, True))


def _s5_scan(u2, bmat, cmat, lam, *, tc, nb):
    rows, dw = u2.shape
    n_chunks = rows // (tc * nb)

    def chunk(d, c):
        return jnp.where(d == 0, c, n_chunks - 1 - c)

    return pl.pallas_call(
        functools.partial(_s5_body, tc=tc, nb=nb),
        grid=(2, n_chunks),
        in_specs=[pl.BlockSpec((tc * nb, dw), lambda d, c: (chunk(d, c), 0)),
                  pl.BlockSpec((None, dw, 2 * S5_LANES), lambda d, c: (d, 0, 0)),
                  pl.BlockSpec((None, 2 * S5_LANES, dw), lambda d, c: (d, 0, 0)),
                  pl.BlockSpec((None, 2, S5_LANES), lambda d, c: (d, 0, 0))],
        out_specs=pl.BlockSpec((None, tc * nb, dw), lambda d, c: (d, chunk(d, c), 0)),
        out_shape=jax.ShapeDtypeStruct((2, rows, dw), F32),
        scratch_shapes=[pltpu.VMEM((tc * nb, 2 * S5_LANES), F32),
                        pltpu.VMEM((tc * nb, 2 * S5_LANES), F32),
                        pltpu.VMEM((nb, 2 * S5_LANES), F32)],
        compiler_params=_cparams(("arbitrary", "arbitrary")),
        name="s5_scan",
    )(u2, bmat, cmat, lam)


def _s5_glu_body(y_ref, u_ref, dskip_ref, w_ref, b_ref, o_ref):
    y = y_ref[0] + y_ref[1] + u_ref[...] * dskip_ref[...]
    z = jax.nn.gelu(y)
    gate = jax.nn.sigmoid(_dot(z.astype(BF16), w_ref[...]) + b_ref[...])
    o_ref[...] = (z * gate).astype(o_ref.dtype)


def _s5_glu(y, u2, dskip, w, bias, *, tr):
    rows, dw = u2.shape
    return pl.pallas_call(
        _s5_glu_body,
        grid=(rows // tr,),
        in_specs=[pl.BlockSpec((2, tr, dw), lambda i: (0, i, 0)),
                  pl.BlockSpec((tr, dw), lambda i: (i, 0)),
                  _resident((1, dw)), _resident(w.shape), _resident((1, dw))],
        out_specs=pl.BlockSpec((tr, dw), lambda i: (i, 0)),
        out_shape=jax.ShapeDtypeStruct((rows, dw), BF16),
        compiler_params=_cparams(("arbitrary",)),
        name="s5_glu",
    )(y, u2, dskip, w, bias)


def _rotary_tables(positions, rot_dim, lane_of_first, period):
    half = rot_dim // 2
    inv_freq = ROPE_THETA ** (-jnp.arange(0, rot_dim, 2, dtype=F32) / rot_dim)
    ang = positions.astype(F32)[..., None] * inv_freq
    cos_h, sin_h = jnp.cos(ang), jnp.sin(ang)
    zero_h = jnp.zeros_like(sin_h)

    def on_lanes(rot_block, fill):
        pieces, lane = [], 0
        while lane < LANES:
            if lane >= lane_of_first and (lane - lane_of_first) % period == 0:
                pieces.append(rot_block)
                lane += rot_dim
            else:
                nxt = lane_of_first if lane < lane_of_first else min(
                    LANES, lane_of_first + -(-(lane - lane_of_first) // period) * period)
                pieces.append(jnp.full(rot_block.shape[:-1] + (nxt - lane,), fill, F32))
                lane = nxt
        return jnp.concatenate(pieces, axis=-1)

    cos = on_lanes(jnp.concatenate([cos_h, cos_h], axis=-1), 1.0)
    s_up = on_lanes(jnp.concatenate([-sin_h, zero_h], axis=-1), 0.0)
    s_dn = on_lanes(jnp.concatenate([zero_h, sin_h], axis=-1), 0.0)
    return cos, s_up, s_dn


def _dft_mats(n):
    ang = 2.0 * np.pi * np.outer(np.arange(n), np.arange(n)) / n
    return np.cos(ang), np.sin(ang)


def _block_diag(blocks):
    g, r, c = blocks.shape
    eye = jnp.eye(g, dtype=blocks.dtype)
    return jnp.einsum("grc,gh->grhc", blocks, eye).reshape(g * r, g * c)


def _pick_tile(n, pref):
    t = min(n, pref)
    while n % t:
        t //= 2
    return t


def kernel(x, positions, ffn1_norm, ffn1_w_gate, ffn1_w_up, ffn1_w_down, mix_norm, ffn2_norm, ffn2_w_gate, ffn2_w_up, ffn2_w_down, ab_w_in, ab_w_out, a_q_norm, a_k_norm, b_w_mix, cd_w_in, cd_w_out, c_q_lat_norm, c_w_q_up, c_kv_lat_norm, c_w_kv_up, c_q_norm, c_k_norm, d_lam_re, d_lam_im, d_log_step, d_b_re, d_b_im, d_c_re, d_c_im, d_skip, d_w_glu, d_b_glu):
    bsz, seq, d_model = x.shape
    depth = ffn1_norm.shape[0]
    tokens = bsz * seq
    ts = _pick_tile(seq, 512)
    assert bsz % SUBLANES == 0
    tm = _pick_tile(seq, 512)
    n1 = seq // FFT_N2
    row = lambda v: v.reshape(1, -1).astype(F32)

    cos_a, sup_a, sdn_a = _rotary_tables(positions, ROT_DIM_A, 0, HEAD_DIM)
    cos_c, sup_c, sdn_c = _rotary_tables(positions, C_ROPE_DIM, C_NOPE_DIM, LANES)

    cc, sc_ = _dft_mats(FNET_GROUP_DIM)
    norm = 1.0 / math.sqrt(seq * FNET_GROUP_DIM)
    eye_g = np.eye(FNET_GROUPS)
    dcos = jnp.asarray(np.kron(eye_g, cc) * norm, BF16)
    dsin = jnp.asarray(np.kron(eye_g, -sc_) * norm, BF16)
    c1, s1 = _dft_mats(n1)
    f1 = jnp.asarray(np.block([[c1, s1], [-s1, c1]]), BF16)
    c2, s2 = _dft_mats(FFT_N2)
    f2 = jnp.asarray(np.concatenate([c2, s2], axis=1), BF16)
    tw_ang = 2.0 * np.pi * np.outer(np.arange(n1), np.arange(FFT_N2)) / seq
    twc = jnp.repeat(jnp.asarray(np.cos(tw_ang), F32), B_WIDTH, axis=1)
    tws = jnp.repeat(jnp.asarray(np.sin(tw_ang), F32), B_WIDTH, axis=1)
    head_mean = jnp.asarray(np.kron(np.eye(N_HEADS_A), np.full((HEAD_DIM, HEAD_DIM), 1.0 / HEAD_DIM)), BF16)

    ffn1_w = [w.astype(BF16) for w in (ffn1_w_gate, ffn1_w_up, ffn1_w_down)]
    ffn2_w = [w.astype(BF16) for w in (ffn2_w_gate, ffn2_w_up, ffn2_w_down)]

    x2 = x.reshape(tokens, d_model)
    for layer in range(depth):
        i = layer // 2
        x2 = _ffn(x2, row(ffn1_norm[layer]), *[w[layer] for w in ffn1_w], tm=tm)
        x3 = x2.reshape(bsz, seq, d_model)
        if layer % 2 == 0:
            mbd = _block_diag(b_w_mix[i]).astype(BF16)
            q, k, v, zr, zi = _even_in(
                x3, row(mix_norm[layer]), ab_w_in[i].astype(BF16), head_mean,
                row(jnp.tile(a_q_norm[i], N_HEADS_A)), row(jnp.tile(a_k_norm[i], N_HEADS_A)),
                cos_a, sup_a, sdn_a, dcos, dsin, mbd, ts=ts)
            a_out = _dilated_sh(q, k, v)
            shp = (bsz, n1, FFT_N2, B_WIDTH)
            yr, yi = _fft1(zr.reshape(shp), zi.reshape(shp), f1, twc, tws)
            shp2 = (bsz, n1 * FFT_N2, B_WIDTH)
            spec = _fft2(yr.reshape(shp2), yi.reshape(shp2), f2)
            w_out = ab_w_out[i].astype(BF16)
            mixer = (a_out.reshape(tokens, A_WIDTH), spec.reshape(tokens, B_WIDTH),
                     w_out[:A_WIDTH], w_out[A_WIDTH:],
                     pl.BlockSpec((tm, B_WIDTH), lambda r: (r, 0)))
        else:
            half = C_ROPE_DIM // 2
            rot_cols = lambda w: jnp.concatenate([-w[..., half:], w[..., :half]], axis=-1)
            swap_halves = lambda w: jnp.concatenate([w[..., half:], w[..., :half]], axis=-1)
            rope_slab = lambda w: jnp.pad(
                w, [(0, 0)] * (w.ndim - 1) + [(C_NOPE_DIM, LANES - C_QK_DIM)])
            w_in = cd_w_in[i]
            pe_off = C_Q_RANK + C_KV_RANK
            w_pe = w_in[:, pe_off:pe_off + C_ROPE_DIM]
            w_in_p = jnp.concatenate([w_in[:, :pe_off], rope_slab(w_pe), rope_slab(rot_cols(w_pe)),
                                      w_in[:, pe_off + C_ROPE_DIM:]], axis=1).astype(BF16)
            pad_h = lambda w, keep: jnp.pad(
                w, ((0, 0), (0, 0), (0, LANES - keep))).reshape(w.shape[0], N_HEADS_C * LANES)
            wq = c_w_q_up[i].reshape(C_Q_RANK, N_HEADS_C, C_QK_DIM)
            wq_p = jnp.concatenate(
                [pad_h(wq, C_QK_DIM),
                 rope_slab(rot_cols(wq[..., C_NOPE_DIM:])).reshape(C_Q_RANK, N_HEADS_C * LANES)],
                axis=1).astype(BF16)
            wkv = c_w_kv_up[i].reshape(C_KV_RANK, N_HEADS_C, C_NOPE_DIM + C_V_DIM)
            wkv_p = jnp.concatenate(
                [pad_h(wkv[:, :, :C_NOPE_DIM], C_NOPE_DIM),
                 wkv[:, :, C_NOPE_DIM:].reshape(C_KV_RANK, N_HEADS_C * C_V_DIM)], axis=1).astype(BF16)
            gains = lambda gvec: jnp.stack(
                [jnp.pad(gvec, (0, LANES - C_QK_DIM)),
                 rope_slab(swap_halves(gvec[C_NOPE_DIM:]))]).astype(F32)
            q, k, v, u_t = _odd_in(
                x3, row(mix_norm[layer]), w_in_p, row(c_q_lat_norm[i]), wq_p,
                row(c_kv_lat_norm[i]), wkv_p, gains(c_q_norm[i]), gains(c_k_norm[i]),
                cos_c, sdn_c - sup_c, ts=ts)
            c_out = _flash(q, k, jnp.swapaxes(v, 1, 2), tq=_pick_tile(seq, 256),
                           tk=_pick_tile(seq, 256), n_q=2)

            lam = lax.complex(d_lam_re[i].astype(F32), d_lam_im[i].astype(F32))
            step = jnp.exp(d_log_step[i].astype(F32))[..., None]
            lam_bar = jnp.exp(lam * step)
            b_bar = ((lam_bar - 1.0) / lam)[..., None] * lax.complex(
                d_b_re[i].astype(F32), d_b_im[i].astype(F32))
            to_bd = lambda m: jnp.stack([_block_diag(m[d]) for d in range(2)])
            b_t = jnp.swapaxes(b_bar, -1, -2)
            bmat = jnp.concatenate([to_bd(jnp.real(b_t)), to_bd(jnp.imag(b_t))], axis=-1).astype(BF16)
            c_re_t = jnp.swapaxes(d_c_re[i].astype(F32), -1, -2)
            c_im_t = jnp.swapaxes(d_c_im[i].astype(F32), -1, -2)
            cmat = jnp.concatenate([to_bd(c_re_t), to_bd(-c_im_t)], axis=-2).astype(BF16)
            lam2 = jnp.stack([jnp.real(lam_bar).reshape(2, S5_LANES),
                              jnp.imag(lam_bar).reshape(2, S5_LANES)], axis=1)

            u2 = u_t.reshape(seq * bsz, D_WIDTH)
            tc = _pick_tile(seq, 128)
            y = _s5_scan(u2, bmat, cmat, lam2, tc=tc, nb=bsz)
            d2 = _s5_glu(y, u2, row(d_skip[i]), d_w_glu[i].astype(BF16), row(d_b_glu[i]),
                         tr=_pick_tile(seq * bsz, 1024))
            d_out = d2.reshape(seq, bsz * D_WIDTH)
            w_out = cd_w_out[i].astype(BF16)
            tiles_per_seq = seq // tm
            mixer = (c_out.reshape(tokens, N_HEADS_C * C_V_DIM), d_out,
                     w_out[:N_HEADS_C * C_V_DIM], w_out[N_HEADS_C * C_V_DIM:],
                     pl.BlockSpec((tm, D_WIDTH),
                                  lambda r: (r % tiles_per_seq, r // tiles_per_seq)))
        x2 = _ffn(x2, row(ffn2_norm[layer]), *[w[layer] for w in ffn2_w], tm=tm, mixer=mixer)
    return x2.reshape(bsz, seq, d_model)
```

```python
import functools
import math

import jax
import jax.numpy as jnp
import numpy as np
from jax import lax
from jax.experimental import pallas as pl
from jax.experimental.pallas import tpu as pltpu

F32 = jnp.float32
BF16 = jnp.bfloat16

HEAD_DIM = 64
N_HEADS_A = 8
ROT_DIM_A = 16
DILATIONS = (1, 4, 16)
RADIUS = 64
FNET_GROUPS = 4
FNET_GROUP_DIM = 64
N_HEADS_C = 8
C_NOPE_DIM = 64
C_ROPE_DIM = 32
C_V_DIM = 64
C_QK_DIM = C_NOPE_DIM + C_ROPE_DIM
C_Q_RANK = 256
C_KV_RANK = 128
S5_GROUPS = 16
S5_GROUP_DIM = 16
S5_STATE = 64
ROPE_THETA = 500000.0
NORM_EPS = 1e-6
NEG_INF = -1e30
A_WIDTH = N_HEADS_A * HEAD_DIM
B_WIDTH = FNET_GROUPS * FNET_GROUP_DIM
D_WIDTH = S5_GROUPS * S5_GROUP_DIM
S5_LANES = S5_GROUPS * S5_STATE
S5_BLOCK_ROWS = 128

LANES = 128
SUBLANES = 8
VMEM_LIMIT_BYTES = 56 * 1024 * 1024


def _cparams(sem):
    return pltpu.CompilerParams(dimension_semantics=sem, vmem_limit_bytes=VMEM_LIMIT_BYTES)


def _resident(shape):
    nd = len(shape)
    return pl.BlockSpec(shape, lambda *_: (0,) * nd, pipeline_mode=pl.Buffered(1))


def _rms_rows(x, g):
    ms = jnp.mean(x * x, axis=-1, keepdims=True)
    return x * lax.rsqrt(ms + NORM_EPS) * g


def _dot(a, b):
    return jnp.dot(a, b, preferred_element_type=F32)


FFN_CHUNK = 256


def _ffn_body(*refs, n_chunks, fused_mixer):
    if fused_mixer:
        x_ref, a_ref, b_ref, wa_ref, wb_ref, g_ref, wg_ref, wu_ref, wd_ref, o_ref = refs
        x = (x_ref[...] + _dot(a_ref[...], wa_ref[...])
             + _dot(b_ref[...].astype(BF16), wb_ref[...]))
    else:
        x_ref, g_ref, wg_ref, wu_ref, wd_ref, o_ref = refs
        x = x_ref[...]
    h = _rms_rows(x, g_ref[...]).astype(BF16)
    acc = jnp.zeros(x.shape, F32)
    for c in range(n_chunks):
        sl = pl.ds(c * FFN_CHUNK, FFN_CHUNK)
        gate = _dot(h, wg_ref[:, sl])
        up = _dot(h, wu_ref[:, sl])
        act = (gate * jax.nn.sigmoid(gate) * up).astype(BF16)
        acc = acc + _dot(act, wd_ref[sl, :])
    o_ref[...] = x + 0.5 * acc


def _ffn(x2, g, wg, wu, wd, *, tm, mixer=None):
    t, d = x2.shape
    f = wg.shape[1]
    assert t % tm == 0 and f % FFN_CHUNK == 0
    rows = lambda w: pl.BlockSpec((tm, w), lambda i: (i, 0))
    weights = [_resident((1, d)), _resident((d, f)), _resident((d, f)), _resident((f, d))]
    if mixer is None:
        in_specs, args = [rows(d)] + weights, (x2, g, wg, wu, wd)
    else:
        a, b, wa, wb, b_spec = mixer
        in_specs = [rows(d), rows(a.shape[1]), b_spec, _resident(wa.shape), _resident(wb.shape)] + weights
        args = (x2, a, b, wa, wb, g, wg, wu, wd)
    return pl.pallas_call(
        functools.partial(_ffn_body, n_chunks=f // FFN_CHUNK, fused_mixer=mixer is not None),
        grid=(t // tm,),
        in_specs=in_specs,
        out_specs=rows(d),
        out_shape=jax.ShapeDtypeStruct((t, d), F32),
        compiler_params=_cparams(("arbitrary",)),
        name="ffn_mix" if mixer is not None else "ffn",
    )(*args)


def _rotary_block(t, cos, s_up, s_dn, shift):
    return (t * cos + pltpu.roll(t, LANES - shift, axis=1) * s_up
            + pltpu.roll(t, shift, axis=1) * s_dn)


def _even_in_body(x_ref, g_ref, w_ref, pm_ref, gq_ref, gk_ref, cos_ref, sup_ref, sdn_ref,
                  dcos_ref, dsin_ref, mbd_ref,
                  q_ref, k_ref, v_ref, zr_ref, zi_ref, wc_scr):
    first = jnp.logical_and(pl.program_id(0) == 0, pl.program_id(1) == 0)

    @pl.when(first)
    def _():
        mbd = mbd_ref[...]
        wc_scr[:, :B_WIDTH] = _dot(dcos_ref[...], mbd).astype(BF16)
        wc_scr[:, B_WIDTH:] = _dot(dsin_ref[...], mbd).astype(BF16)

    h = _rms_rows(x_ref[...], g_ref[...]).astype(BF16)
    z = _dot(h, w_ref[...])
    cos, s_up, s_dn = cos_ref[...], sup_ref[...], sdn_ref[...]
    pm = pm_ref[...]

    def head_norm_rot(t, gain, scale):
        ms = _dot((t * t).astype(BF16), pm)
        tn = t * lax.rsqrt(ms + NORM_EPS) * gain
        blocks = [_rotary_block(tn[:, c * LANES:(c + 1) * LANES], cos, s_up, s_dn, ROT_DIM_A // 2)
                  for c in range(A_WIDTH // LANES)]
        out = jnp.concatenate(blocks, axis=1)
        return out * scale if scale != 1.0 else out

    q_ref[...] = head_norm_rot(z[:, :A_WIDTH], gq_ref[...], HEAD_DIM ** -0.5 * math.log2(math.e))
    k_ref[...] = head_norm_rot(z[:, A_WIDTH:2 * A_WIDTH], gk_ref[...], 1.0)
    v_ref[...] = z[:, 2 * A_WIDTH:3 * A_WIDTH]
    zz = _dot(z[:, 3 * A_WIDTH:].astype(BF16), wc_scr[...])
    zr_ref[...] = zz[:, :B_WIDTH]
    zi_ref[...] = zz[:, B_WIDTH:]


def _even_in(x, g, w_in, pm, gq, gk, cos, s_up, s_dn, dcos, dsin, mbd, *, ts):
    b, s, d = x.shape
    tok = lambda w: pl.BlockSpec((None, ts, w), lambda bi, i: (bi, i, 0))
    return pl.pallas_call(
        _even_in_body,
        grid=(b, s // ts),
        in_specs=[tok(d), _resident((1, d)), _resident(w_in.shape), _resident(pm.shape),
                  _resident((1, A_WIDTH)), _resident((1, A_WIDTH)),
                  tok(LANES), tok(LANES), tok(LANES),
                  _resident(dcos.shape), _resident(dsin.shape), _resident(mbd.shape)],
        out_specs=[tok(A_WIDTH), tok(A_WIDTH), tok(A_WIDTH), tok(B_WIDTH), tok(B_WIDTH)],
        out_shape=[jax.ShapeDtypeStruct((b, s, A_WIDTH), F32)] * 3
        + [jax.ShapeDtypeStruct((b, s, B_WIDTH), F32)] * 2,
        scratch_shapes=[pltpu.VMEM((B_WIDTH, 2 * B_WIDTH), BF16)],
        compiler_params=_cparams(("arbitrary", "arbitrary")),
        name="even_in",
    )(x, g, w_in, pm, gq, gk, cos, s_up, s_dn, dcos, dsin, mbd)


DIL_ONES_ROWS = 16
SH_TQ = LANES
SH_TK = 2 * LANES
SH_QPAD = RADIUS
SH_KPAD = LANES
SH_PAIR = 4


def _dilated_sh_body(q_ref, k_ref, v_ref, o_ref, qd, kd, vt, o_st, l_st, out_scr, s_scr, p_scr,
                     bias_scr, *, seq):
    lane = lax.broadcasted_iota(jnp.int32, (1, LANES), 1)
    head_lanes = (lane < HEAD_DIM, lane >= HEAD_DIM)
    kk = lax.broadcasted_iota(jnp.int32, (SH_TK, SH_TQ), 0)
    qq = lax.broadcasted_iota(jnp.int32, (SH_TK, SH_TQ), 1)
    band = jnp.abs(kk - qq - (SH_KPAD - SH_QPAD)) <= RADIUS
    key_ok = (kk >= SH_KPAD, kk >= 0, kk < SH_KPAD)
    for idx in range(3):
        bias_scr[idx] = jnp.where(jnp.logical_and(band, key_ok[idx]), 0.0, NEG_INF)
    ones = jnp.ones((DIL_ONES_ROWS, SH_TK), BF16)
    max_dil = max(DILATIONS)
    state_pad = SH_QPAD * max_dil

    order = tuple(reversed(DILATIONS))
    for dil in order:
        first, last = dil == order[0], dil == order[-1]
        sub_len = seq // dil
        chunks = sub_len // LANES
        tiles_per_class = chunks + 1
        n_tiles = dil * tiles_per_class
        n_items = -(-n_tiles // SH_PAIR)
        assert last or n_tiles % SH_PAIR == 0
        q_stride = sub_len + 2 * SH_QPAD
        k_stride = sub_len + 2 * SH_KPAD

        def zero_pads(res, carry, sub_len=sub_len, q_stride=q_stride, k_stride=k_stride):
            zq = jnp.zeros((SH_QPAD, LANES), BF16)
            zk = jnp.zeros((SH_KPAD, LANES), BF16)
            for off in (0, SH_QPAD + sub_len):
                qd[pl.ds(pl.multiple_of(res * q_stride + off, SH_QPAD), SH_QPAD), :] = zq
            for off in (0, SH_KPAD + sub_len):
                pad = pl.ds(pl.multiple_of(res * k_stride + off, SH_KPAD), SH_KPAD)
                kd[pad, :] = zk
                vt[:, pad] = jnp.zeros((LANES, SH_KPAD), BF16)
            return carry

        lax.fori_loop(0, dil, zero_pads, 0)

        def prepare(idx, carry, dil=dil, chunks=chunks, q_stride=q_stride, k_stride=k_stride):
            res = idx // chunks
            c0 = (idx - res * chunks) * LANES
            if dil == 1:
                src = pl.ds(pl.multiple_of(c0, LANES), LANES)
            else:
                src = pl.ds(res + dil * c0, LANES, stride=dil)
            k_dst = pl.ds(pl.multiple_of(res * k_stride + SH_KPAD + c0, LANES), LANES)
            qd[pl.ds(pl.multiple_of(res * q_stride + SH_QPAD + c0, SH_QPAD), LANES), :] = (
                q_ref[src, :].astype(BF16))
            kd[k_dst, :] = k_ref[src, :].astype(BF16)
            vt[:, k_dst] = v_ref[src, :].T.astype(BF16)
            return carry

        lax.fori_loop(0, seq // LANES, prepare, 0, unroll=4)

        def geometry(tile, tiles_per_class=tiles_per_class, n_tiles=n_tiles):
            tile = jnp.minimum(tile, n_tiles - 1)
            res = tile // tiles_per_class
            i = tile - res * tiles_per_class
            edge = (i > 0).astype(jnp.int32) + (i == tiles_per_class - 1).astype(jnp.int32)
            return res, i, edge

        def scores(item, slot, geometry=geometry, q_stride=q_stride, k_stride=k_stride):
            col_max = []
            for u in range(SH_PAIR):
                res, i, edge = geometry(item * SH_PAIR + u)
                k_win = kd[pl.ds(pl.multiple_of(res * k_stride + i * LANES, LANES), SH_TK), :]
                q2 = qd[pl.ds(pl.multiple_of(res * q_stride + i * LANES, LANES), SH_TQ), :]
                bias = bias_scr[edge]
                for h in range(2):
                    qh = jnp.where(head_lanes[h], q2, jnp.zeros_like(q2))
                    s = lax.dot_general(k_win, qh, (((1,), (1,)), ((), ())),
                                        preferred_element_type=F32) + bias
                    s_scr[slot, u, h] = s
                    col_max.append(jnp.max(s, axis=0, keepdims=True))
            return tuple(col_max)

        def exponentials(slot, col_max):
            for u in range(SH_PAIR):
                for h in range(2):
                    p_scr[slot, u, h] = jnp.exp2(s_scr[slot, u, h] - col_max[2 * u + h]).astype(BF16)

        def values(item, slot, col_max, geometry=geometry, dil=dil, k_stride=k_stride,
                   first=first, last=last):
            for u in range(SH_PAIR):
                res, i, _ = geometry(item * SH_PAIR + u)
                cols = pl.ds(pl.multiple_of(res * k_stride + i * LANES, LANES), SH_TK)
                o_t, l_t = [], []
                for h in range(2):
                    vth = jnp.concatenate([vt[h * HEAD_DIM:(h + 1) * HEAD_DIM, cols], ones], axis=0)
                    acc = _dot(vth, p_scr[slot, u, h])
                    denom = acc[HEAD_DIM:HEAD_DIM + 1]
                    o_t.append(acc[:HEAD_DIM] / denom)
                    l_t.append(jnp.broadcast_to(col_max[2 * u + h] + jnp.log2(denom),
                                                (HEAD_DIM, SH_TQ)))
                o_new = jnp.concatenate(o_t, axis=0).T
                l_new = jnp.concatenate(l_t, axis=0).T
                start = state_pad + res + dil * (i * LANES - SH_QPAD)
                if dil == 1:
                    st_rows = pl.ds(pl.multiple_of(start, SH_QPAD), SH_TQ)
                else:
                    st_rows = pl.ds(start, SH_TQ, stride=dil)
                if not first:
                    o_old, l_old = o_st[st_rows, :], l_st[st_rows, :]
                    l_max = jnp.maximum(l_old, l_new)
                    w_old, w_new = jnp.exp2(l_old - l_max), jnp.exp2(l_new - l_max)
                    w_sum = w_old + w_new
                    o_new = (w_old * o_old + w_new * o_new) / w_sum
                    l_new = l_max + jnp.log2(w_sum)
                if last:
                    out_scr[pl.ds(pl.multiple_of(i * LANES, LANES), SH_TQ), :] = o_new.astype(BF16)
                else:
                    o_st[st_rows, :] = o_new
                    l_st[st_rows, :] = l_new

        def tick(t, parity, carry, scores=scores, exponentials=exponentials, values=values):
            cm_prev, cm_prev2 = carry
            cm_new = scores(t, parity)
            exponentials(1 - parity, cm_prev)
            values(t - 2, parity, cm_prev2)
            return cm_new, cm_prev

        cm0 = scores(0, 0)
        cm1 = scores(1, 1)
        exponentials(0, cm0)

        def two_ticks(j, carry, tick=tick):
            t = 2 * j + 2
            return tick(t + 1, 1, tick(t, 0, carry))

        loops = (n_items - 2) // 2
        carry = lax.fori_loop(0, loops, two_ticks, (cm1, cm0))
        for t in range(2 + 2 * loops, n_items):
            carry = tick(t, t % 2, carry)
        cm_last, cm_prev = carry
        exponentials((n_items - 1) % 2, cm_last)
        values(n_items - 2, n_items % 2, cm_prev)
        values(n_items - 1, (n_items - 1) % 2, cm_last)

    o_ref[...] = out_scr[SH_QPAD:SH_QPAD + seq, :]


def _dilated_sh(q, k, v):
    b, s, w = q.shape
    max_dil = max(DILATIONS)
    assert DILATIONS[0] == 1 and s % (max_dil * LANES) == 0
    spec = pl.BlockSpec((None, s, LANES), lambda bi, p: (bi, 0, p))
    return pl.pallas_call(
        functools.partial(_dilated_sh_body, seq=s),
        grid=(b, w // LANES),
        in_specs=[pl.BlockSpec((None, s, LANES), lambda bi, p: (bi, 0, p),
                               pipeline_mode=pl.Buffered(1))] * 3,
        out_specs=spec,
        out_shape=jax.ShapeDtypeStruct((b, s, w), BF16),
        scratch_shapes=[pltpu.VMEM((s + 2 * SH_QPAD * max_dil, LANES), BF16),
                        pltpu.VMEM((s + 2 * SH_KPAD * max_dil, LANES), BF16),
                        pltpu.VMEM((LANES, s + 2 * SH_KPAD * max_dil), BF16),
                        pltpu.VMEM((s + 2 * SH_QPAD * max_dil, LANES), F32),
                        pltpu.VMEM((s + 2 * SH_QPAD * max_dil, LANES), F32),
                        pltpu.VMEM((s + 2 * SH_QPAD, LANES), BF16),
                        pltpu.VMEM((2, SH_PAIR, 2, SH_TK, SH_TQ), F32),
                        pltpu.VMEM((2, SH_PAIR, 2, SH_TK, SH_TQ), BF16),
                        pltpu.VMEM((3, SH_TK, SH_TQ), F32)],
        compiler_params=_cparams(("arbitrary", "arbitrary")),
        name="dilated",
    )(q, k, v)


FFT_N2 = 128


FFT_ROWS = SUBLANES


def _fft1_body(zr_ref, zi_ref, f_ref, twc_ref, tws_ref, yr_ref, yi_ref, *, n1):
    f = f_ref[...]
    for j in range(FFT_ROWS):
        zz = jnp.concatenate([zr_ref[:, j, :], zi_ref[:, j, :]], axis=0).astype(BF16)
        y = _dot(f, zz)
        yr, yi = y[:n1], y[n1:]
        cols = slice(j * B_WIDTH, (j + 1) * B_WIDTH)
        c, s = twc_ref[:, cols], tws_ref[:, cols]
        yr_ref[:, j, :] = yr * c + yi * s
        yi_ref[:, j, :] = yi * c - yr * s


def _fft1(zr, zi, fmat, twc, tws):
    b, n1, n2, w = zr.shape
    dat = pl.BlockSpec((None, n1, FFT_ROWS, w), lambda bi, j: (bi, 0, j, 0))
    tw = pl.BlockSpec((n1, FFT_ROWS * w), lambda bi, j: (0, j))
    return pl.pallas_call(
        functools.partial(_fft1_body, n1=n1),
        grid=(b, n2 // FFT_ROWS),
        in_specs=[dat, dat, _resident(fmat.shape), tw, tw],
        out_specs=[dat, dat],
        out_shape=[jax.ShapeDtypeStruct((b, n1, n2, w), F32)] * 2,
        compiler_params=_cparams(("arbitrary", "arbitrary")),
        name="fft_stage1",
    )(zr, zi, fmat, twc, tws)


def _fft2_body(yr_ref, yi_ref, f_ref, o_ref):
    f = f_ref[...]
    for j in range(FFT_ROWS):
        rs = pl.ds(j * FFT_N2, FFT_N2)
        yy = jnp.concatenate([yr_ref[rs, :], yi_ref[rs, :]], axis=0).astype(BF16)
        o_ref[:, j, :] = _dot(f, yy)


def _fft2(yr, yi, fmat):
    b, rows, w = yr.shape
    n1 = rows // FFT_N2
    dat = pl.BlockSpec((None, FFT_ROWS * FFT_N2, w), lambda bi, j: (bi, j, 0))
    return pl.pallas_call(
        _fft2_body,
        grid=(b, n1 // FFT_ROWS),
        in_specs=[dat, dat, _resident(fmat.shape)],
        out_specs=pl.BlockSpec((None, FFT_N2, FFT_ROWS, w), lambda bi, j: (bi, 0, j, 0)),
        out_shape=jax.ShapeDtypeStruct((b, FFT_N2, n1, w), F32),
        compiler_params=_cparams(("arbitrary", "arbitrary")),
        name="fft_stage2",
    )(yr, yi, fmat)


def _odd_in_body(x_ref, g_ref, w_ref, gql_ref, wq_ref, gkvl_ref, wkv_ref, gq_ref, gk_ref,
                 cos_ref, sin_ref, q_ref, k_ref, v_ref, u_ref):
    h = _rms_rows(x_ref[...], g_ref[...]).astype(BF16)
    z = _dot(h, w_ref[...])
    cos, sin = cos_ref[...], sin_ref[...]
    kv_off = C_Q_RANK
    pe_off = kv_off + C_KV_RANK
    rot_off = pe_off + LANES
    u_off = rot_off + LANES
    hw = N_HEADS_C * LANES
    q_lat = _rms_rows(z[:, :C_Q_RANK], gql_ref[...]).astype(BF16)
    kv_lat = _rms_rows(z[:, kv_off:pe_off], gkvl_ref[...]).astype(BF16)
    k_pe = z[:, pe_off:rot_off]
    k_pe_rot = z[:, rot_off:u_off]
    qf = _dot(q_lat, wq_ref[...])
    kvf = _dot(kv_lat, wkv_ref[...])

    def head(t, t_rot, gains, scale):
        ms = jnp.sum(t * t, axis=-1, keepdims=True) * (1.0 / C_QK_DIM)
        r = lax.rsqrt(ms + NORM_EPS) * scale
        return (t * (gains[0:1] * cos) + t_rot * (gains[1:2] * sin)) * r

    gq, gk = gq_ref[...], gk_ref[...]
    for hd in range(N_HEADS_C):
        sl = slice(hd * LANES, (hd + 1) * LANES)
        rot_sl = slice(hw + hd * LANES, hw + (hd + 1) * LANES)
        q_ref[:, sl] = head(qf[:, sl], qf[:, rot_sl], gq,
                            C_QK_DIM ** -0.5 * math.log2(math.e)).astype(BF16)
        k_ref[:, sl] = head(kvf[:, sl] + k_pe, k_pe_rot, gk, 1.0).astype(BF16)
    v_ref[...] = kvf[:, hw:].astype(BF16)
    u_ref[...] = z[:, u_off:]


def _odd_in(x, g, w_in, gql, wq, gkvl, wkv, gq, gk, cos, sin, *, ts):
    b, s, d = x.shape
    tok = lambda w: pl.BlockSpec((None, ts, w), lambda bi, i: (bi, i, 0))
    hw = N_HEADS_C * LANES
    return pl.pallas_call(
        _odd_in_body,
        grid=(b, s // ts),
        in_specs=[tok(d), _resident((1, d)), _resident(w_in.shape), _resident(gql.shape),
                  _resident(wq.shape), _resident(gkvl.shape), _resident(wkv.shape),
                  _resident(gq.shape), _resident(gk.shape), tok(LANES), tok(LANES)],
        out_specs=[tok(hw), tok(hw), tok(N_HEADS_C * C_V_DIM),
                   pl.BlockSpec((ts, D_WIDTH), lambda bi, i: (i, bi))],
        out_shape=[jax.ShapeDtypeStruct((b, s, hw), BF16), jax.ShapeDtypeStruct((b, s, hw), BF16),
                   jax.ShapeDtypeStruct((b, s, N_HEADS_C * C_V_DIM), BF16),
                   jax.ShapeDtypeStruct((s, b * D_WIDTH), F32)],
        compiler_params=_cparams(("arbitrary", "arbitrary")),
        name="odd_in",
    )(x, g, w_in, gql, wq, gkvl, wkv, gq, gk, cos, sin)


FLASH_ONES_ROWS = 16


def _flash_body(q_ref, k_ref, vt_ref, o_ref, s_scr, p_scr, *, tq, tk):
    n_q = q_ref.shape[0] // tq
    n_kv = k_ref.shape[0] // tk
    heads = ((0, 0), (LANES, C_V_DIM))
    ones = jnp.ones((FLASH_ONES_ROWS, tk), BF16)
    items = [(qi, kj) for qi in range(n_q) for kj in range(n_kv)]

    def scores(item, slot):
        qi, kj = item
        col_max = []
        for h, (off, _) in enumerate(heads):
            s = lax.dot_general(k_ref[kj * tk:(kj + 1) * tk, off:off + LANES],
                                q_ref[qi * tq:(qi + 1) * tq, off:off + LANES],
                                (((1,), (1,)), ((), ())), preferred_element_type=F32)
            s_scr[slot, h] = s
            col_max.append(jnp.max(s, axis=0, keepdims=True))
        return col_max

    def softmax(slot, m2, col_max2):
        m_out, alpha_out = [], []
        for h in range(2):
            m_new = jnp.maximum(m2[h], col_max2[h])
            p_scr[slot, h] = jnp.exp2(s_scr[slot, h] - m_new).astype(BF16)
            alpha_out.append(jnp.exp2(m2[h] - m_new))
            m_out.append(m_new)
        return m_out, alpha_out

    def accumulate(item, slot, acc2, alpha2):
        _, kj = item
        out = []
        for h, (_, voff) in enumerate(heads):
            vt = jnp.concatenate([vt_ref[voff:voff + C_V_DIM, kj * tk:(kj + 1) * tk], ones], axis=0)
            out.append(acc2[h] * alpha2[h] + _dot(vt, p_scr[slot, h]))
        return out

    m0 = jnp.full((1, tq), NEG_INF, F32)
    a0 = jnp.zeros((C_V_DIM + FLASH_ONES_ROWS, tq), F32)
    m_state = [[m0, m0] for _ in range(n_q)]
    acc_state = [[a0, a0] for _ in range(n_q)]
    col, alpha = {}, {}
    for t in range(len(items) + 2):
        if t < len(items):
            col[t] = scores(items[t], t % 2)
        if 0 <= t - 1 < len(items):
            qi = items[t - 1][0]
            m_state[qi], alpha[t - 1] = softmax((t - 1) % 2, m_state[qi], col.pop(t - 1))
        if 0 <= t - 2 < len(items):
            qi, kj = items[t - 2]
            acc_state[qi] = accumulate(items[t - 2], t % 2, acc_state[qi], alpha.pop(t - 2))
            if kj == n_kv - 1:
                out_t = jnp.concatenate(
                    [acc[:C_V_DIM] / acc[C_V_DIM:C_V_DIM + 1] for acc in acc_state[qi]], axis=0)
                o_ref[qi * tq:(qi + 1) * tq, :] = out_t.T.astype(o_ref.dtype)


def _flash(q, k, vt, *, tq, tk, n_q):
    b, s, hw = q.shape
    pairs = hw // (2 * LANES)
    bq = tq * n_q
    assert s % bq == 0 and s % tk == 0
    return pl.pallas_call(
        functools.partial(_flash_body, tq=tq, tk=tk),
        grid=(b, pairs, s // bq),
        in_specs=[pl.BlockSpec((None, bq, 2 * LANES), lambda bi, p, i: (bi, i, p)),
                  pl.BlockSpec((None, s, 2 * LANES), lambda bi, p, i: (bi, 0, p)),
                  pl.BlockSpec((None, 2 * C_V_DIM, s), lambda bi, p, i: (bi, p, 0))],
        out_specs=pl.BlockSpec((None, bq, LANES), lambda bi, p, i: (bi, i, p)),
        out_shape=jax.ShapeDtypeStruct((b, s, pairs * LANES), BF16),
        scratch_shapes=[pltpu.VMEM((2, 2, tk, tq), F32), pltpu.VMEM((2, 2, tk, tq), BF16)],
        compiler_params=_cparams(("arbitrary", "arbitrary", "arbitrary")),
        name="flash",
    )(q, k, vt)


def _s5_body(u_ref, bm_ref, cm_ref, lam_ref, y_ref, bu_scr, xs_scr, st_scr, *, tc, nb):
    direction = pl.program_id(0)

    @pl.when(pl.program_id(1) == 0)
    def _():
        st_scr[...] = jnp.zeros(st_scr.shape, F32)

    a_re = jnp.broadcast_to(lam_ref[0:1, :], (nb, S5_LANES))
    a_im = jnp.broadcast_to(lam_ref[1:2, :], (nb, S5_LANES))
    steps_per_block = S5_BLOCK_ROWS // nb
    n_blocks = tc // steps_per_block

    def run(reverse):
        order = range(n_blocks - 1, -1, -1) if reverse else range(n_blocks)
        for blk in order:
            rows = slice(blk * S5_BLOCK_ROWS, (blk + 1) * S5_BLOCK_ROWS)
            bu_scr[rows, :] = _dot(u_ref[rows, :].astype(BF16), bm_ref[...])
        x_re, x_im = st_scr[:, :S5_LANES], st_scr[:, S5_LANES:]
        for blk in order:
            for t in (range(steps_per_block - 1, -1, -1) if reverse else range(steps_per_block)):
                r0 = blk * S5_BLOCK_ROWS + t * nb
                x_re, x_im = (a_re * x_re - a_im * x_im + bu_scr[r0:r0 + nb, :S5_LANES],
                              a_re * x_im + a_im * x_re + bu_scr[r0:r0 + nb, S5_LANES:])
                xs_scr[r0:r0 + nb, :S5_LANES] = x_re
                xs_scr[r0:r0 + nb, S5_LANES:] = x_im
            rows = slice(blk * S5_BLOCK_ROWS, (blk + 1) * S5_BLOCK_ROWS)
            y_ref[rows, :] = _dot(xs_scr[rows, :].astype(BF16), cm_ref[...])
        st_scr[:, :S5_LANES] = x_re
        st_scr[:, S5_LANES:] = x_im

    pl.when(direction == 0)(functools.partial(run, False))
    pl.when(direction == 1)(functools.partial(run, True))


def _s5_scan(u2, bmat, cmat, lam, *, tc, nb):
    rows, dw = u2.shape
    n_chunks = rows // (tc * nb)

    def chunk(d, c):
        return jnp.where(d == 0, c, n_chunks - 1 - c)

    return pl.pallas_call(
        functools.partial(_s5_body, tc=tc, nb=nb),
        grid=(2, n_chunks),
        in_specs=[pl.BlockSpec((tc * nb, dw), lambda d, c: (chunk(d, c), 0)),
                  pl.BlockSpec((None, dw, 2 * S5_LANES), lambda d, c: (d, 0, 0)),
                  pl.BlockSpec((None, 2 * S5_LANES, dw), lambda d, c: (d, 0, 0)),
                  pl.BlockSpec((None, 2, S5_LANES), lambda d, c: (d, 0, 0))],
        out_specs=pl.BlockSpec((None, tc * nb, dw), lambda d, c: (d, chunk(d, c), 0)),
        out_shape=jax.ShapeDtypeStruct((2, rows, dw), F32),
        scratch_shapes=[pltpu.VMEM((tc * nb, 2 * S5_LANES), F32),
                        pltpu.VMEM((tc * nb, 2 * S5_LANES), F32),
                        pltpu.VMEM((nb, 2 * S5_LANES), F32)],
        compiler_params=_cparams(("arbitrary", "arbitrary")),
        name="s5_scan",
    )(u2, bmat, cmat, lam)


def _s5_glu_body(y_ref, u_ref, dskip_ref, w_ref, b_ref, o_ref):
    y = y_ref[0] + y_ref[1] + u_ref[...] * dskip_ref[...]
    z = jax.nn.gelu(y)
    gate = jax.nn.sigmoid(_dot(z.astype(BF16), w_ref[...]) + b_ref[...])
    o_ref[...] = (z * gate).astype(o_ref.dtype)


def _s5_glu(y, u2, dskip, w, bias, *, tr):
    rows, dw = u2.shape
    return pl.pallas_call(
        _s5_glu_body,
        grid=(rows // tr,),
        in_specs=[pl.BlockSpec((2, tr, dw), lambda i: (0, i, 0)),
                  pl.BlockSpec((tr, dw), lambda i: (i, 0)),
                  _resident((1, dw)), _resident(w.shape), _resident((1, dw))],
        out_specs=pl.BlockSpec((tr, dw), lambda i: (i, 0)),
        out_shape=jax.ShapeDtypeStruct((rows, dw), BF16),
        compiler_params=_cparams(("arbitrary",)),
        name="s5_glu",
    )(y, u2, dskip, w, bias)


def _rotary_tables(positions, rot_dim, lane_of_first, period):
    half = rot_dim // 2
    inv_freq = ROPE_THETA ** (-jnp.arange(0, rot_dim, 2, dtype=F32) / rot_dim)
    ang = positions.astype(F32)[..., None] * inv_freq
    ang = jnp.concatenate([ang, ang], axis=-1)
    cos_r, sin_r = jnp.cos(ang), jnp.sin(ang)
    lane = np.arange(LANES)
    j = (lane - lane_of_first) % period
    in_rot = (lane >= lane_of_first) & (j < rot_dim)
    idx = np.where(in_rot, j, 0)
    cos = jnp.where(in_rot, cos_r[..., idx], 1.0)
    sin = jnp.where(in_rot, sin_r[..., idx], 0.0)
    s_up = jnp.where(in_rot & (j < half), -sin, 0.0)
    s_dn = jnp.where(in_rot & (j >= half), sin, 0.0)
    return cos, s_up, s_dn


def _dft_mats(n):
    ang = 2.0 * np.pi * np.outer(np.arange(n), np.arange(n)) / n
    return np.cos(ang), np.sin(ang)


def _block_diag(blocks):
    g, r, c = blocks.shape
    eye = jnp.eye(g, dtype=blocks.dtype)
    return jnp.einsum("grc,gh->grhc", blocks, eye).reshape(g * r, g * c)


def _pick_tile(n, pref):
    t = min(n, pref)
    while n % t:
        t //= 2
    return t


def kernel(x, positions, ffn1_norm, ffn1_w_gate, ffn1_w_up, ffn1_w_down, mix_norm, ffn2_norm, ffn2_w_gate, ffn2_w_up, ffn2_w_down, ab_w_in, ab_w_out, a_q_norm, a_k_norm, b_w_mix, cd_w_in, cd_w_out, c_q_lat_norm, c_w_q_up, c_kv_lat_norm, c_w_kv_up, c_q_norm, c_k_norm, d_lam_re, d_lam_im, d_log_step, d_b_re, d_b_im, d_c_re, d_c_im, d_skip, d_w_glu, d_b_glu):
    bsz, seq, d_model = x.shape
    depth = ffn1_norm.shape[0]
    tokens = bsz * seq
    ts = _pick_tile(seq, 512)
    assert bsz % SUBLANES == 0
    tm = _pick_tile(seq, 512)
    n1 = seq // FFT_N2
    row = lambda v: v.reshape(1, -1).astype(F32)

    cos_a, sup_a, sdn_a = _rotary_tables(positions, ROT_DIM_A, 0, HEAD_DIM)
    cos_c, sup_c, sdn_c = _rotary_tables(positions, C_ROPE_DIM, C_NOPE_DIM, LANES)

    cc, sc_ = _dft_mats(FNET_GROUP_DIM)
    norm = 1.0 / math.sqrt(seq * FNET_GROUP_DIM)
    eye_g = np.eye(FNET_GROUPS)
    dcos = jnp.asarray(np.kron(eye_g, cc) * norm, BF16)
    dsin = jnp.asarray(np.kron(eye_g, -sc_) * norm, BF16)
    c1, s1 = _dft_mats(n1)
    f1 = jnp.asarray(np.block([[c1, s1], [-s1, c1]]), BF16)
    c2, s2 = _dft_mats(FFT_N2)
    f2 = jnp.asarray(np.concatenate([c2, s2], axis=1), BF16)
    tw_ang = 2.0 * np.pi * np.outer(np.arange(n1), np.arange(FFT_N2)) / seq
    twc = jnp.repeat(jnp.asarray(np.cos(tw_ang), F32), B_WIDTH, axis=1)
    tws = jnp.repeat(jnp.asarray(np.sin(tw_ang), F32), B_WIDTH, axis=1)
    head_mean = jnp.asarray(np.kron(np.eye(N_HEADS_A), np.full((HEAD_DIM, HEAD_DIM), 1.0 / HEAD_DIM)), BF16)

    ffn1_w = [w.astype(BF16) for w in (ffn1_w_gate, ffn1_w_up, ffn1_w_down)]
    ffn2_w = [w.astype(BF16) for w in (ffn2_w_gate, ffn2_w_up, ffn2_w_down)]

    x2 = x.reshape(tokens, d_model)
    for layer in range(depth):
        i = layer // 2
        x2 = _ffn(x2, row(ffn1_norm[layer]), *[w[layer] for w in ffn1_w], tm=tm)
        x3 = x2.reshape(bsz, seq, d_model)
        if layer % 2 == 0:
            mbd = _block_diag(b_w_mix[i]).astype(BF16)
            q, k, v, zr, zi = _even_in(
                x3, row(mix_norm[layer]), ab_w_in[i].astype(BF16), head_mean,
                row(jnp.tile(a_q_norm[i], N_HEADS_A)), row(jnp.tile(a_k_norm[i], N_HEADS_A)),
                cos_a, sup_a, sdn_a, dcos, dsin, mbd, ts=ts)
            a_out = _dilated_sh(q, k, v)
            shp = (bsz, n1, FFT_N2, B_WIDTH)
            yr, yi = _fft1(zr.reshape(shp), zi.reshape(shp), f1, twc, tws)
            shp2 = (bsz, n1 * FFT_N2, B_WIDTH)
            spec = _fft2(yr.reshape(shp2), yi.reshape(shp2), f2)
            w_out = ab_w_out[i].astype(BF16)
            mixer = (a_out.reshape(tokens, A_WIDTH), spec.reshape(tokens, B_WIDTH),
                     w_out[:A_WIDTH], w_out[A_WIDTH:],
                     pl.BlockSpec((tm, B_WIDTH), lambda r: (r, 0)))
        else:
            half = C_ROPE_DIM // 2
            rot_cols = lambda w: jnp.concatenate([-w[..., half:], w[..., :half]], axis=-1)
            swap_halves = lambda w: jnp.concatenate([w[..., half:], w[..., :half]], axis=-1)
            rope_slab = lambda w: jnp.pad(
                w, [(0, 0)] * (w.ndim - 1) + [(C_NOPE_DIM, LANES - C_QK_DIM)])
            w_in = cd_w_in[i]
            pe_off = C_Q_RANK + C_KV_RANK
            w_pe = w_in[:, pe_off:pe_off + C_ROPE_DIM]
            w_in_p = jnp.concatenate([w_in[:, :pe_off], rope_slab(w_pe), rope_slab(rot_cols(w_pe)),
                                      w_in[:, pe_off + C_ROPE_DIM:]], axis=1).astype(BF16)
            pad_h = lambda w, keep: jnp.pad(
                w, ((0, 0), (0, 0), (0, LANES - keep))).reshape(w.shape[0], N_HEADS_C * LANES)
            wq = c_w_q_up[i].reshape(C_Q_RANK, N_HEADS_C, C_QK_DIM)
            wq_p = jnp.concatenate(
                [pad_h(wq, C_QK_DIM),
                 rope_slab(rot_cols(wq[..., C_NOPE_DIM:])).reshape(C_Q_RANK, N_HEADS_C * LANES)],
                axis=1).astype(BF16)
            wkv = c_w_kv_up[i].reshape(C_KV_RANK, N_HEADS_C, C_NOPE_DIM + C_V_DIM)
            wkv_p = jnp.concatenate(
                [pad_h(wkv[:, :, :C_NOPE_DIM], C_NOPE_DIM),
                 wkv[:, :, C_NOPE_DIM:].reshape(C_KV_RANK, N_HEADS_C * C_V_DIM)], axis=1).astype(BF16)
            gains = lambda gvec: jnp.stack(
                [jnp.pad(gvec, (0, LANES - C_QK_DIM)),
                 rope_slab(swap_halves(gvec[C_NOPE_DIM:]))]).astype(F32)
            q, k, v, u_t = _odd_in(
                x3, row(mix_norm[layer]), w_in_p, row(c_q_lat_norm[i]), wq_p,
                row(c_kv_lat_norm[i]), wkv_p, gains(c_q_norm[i]), gains(c_k_norm[i]),
                cos_c, sdn_c - sup_c, ts=ts)
            c_out = _flash(q, k, jnp.swapaxes(v, 1, 2), tq=_pick_tile(seq, 256),
                           tk=_pick_tile(seq, 256), n_q=4)

            lam = lax.complex(d_lam_re[i].astype(F32), d_lam_im[i].astype(F32))
            step = jnp.exp(d_log_step[i].astype(F32))[..., None]
            lam_bar = jnp.exp(lam * step)
            b_bar = ((lam_bar - 1.0) / lam)[..., None] * lax.complex(
                d_b_re[i].astype(F32), d_b_im[i].astype(F32))
            to_bd = lambda m: jnp.stack([_block_diag(m[d]) for d in range(2)])
            b_t = jnp.swapaxes(b_bar, -1, -2)
            bmat = jnp.concatenate([to_bd(jnp.real(b_t)), to_bd(jnp.imag(b_t))], axis=-1).astype(BF16)
            c_re_t = jnp.swapaxes(d_c_re[i].astype(F32), -1, -2)
            c_im_t = jnp.swapaxes(d_c_im[i].astype(F32), -1, -2)
            cmat = jnp.concatenate([to_bd(c_re_t), to_bd(-c_im_t)], axis=-2).astype(BF16)
            lam2 = jnp.stack([jnp.real(lam_bar).reshape(2, S5_LANES),
                              jnp.imag(lam_bar).reshape(2, S5_LANES)], axis=1)

            u2 = u_t.reshape(seq * bsz, D_WIDTH)
            tc = _pick_tile(seq, 128)
            y = _s5_scan(u2, bmat, cmat, lam2, tc=tc, nb=bsz)
            d2 = _s5_glu(y, u2, row(d_skip[i]), d_w_glu[i].astype(BF16), row(d_b_glu[i]),
                         tr=_pick_tile(seq * bsz, 1024))
            d_out = d2.reshape(seq, bsz * D_WIDTH)
            w_out = cd_w_out[i].astype(BF16)
            tiles_per_seq = seq // tm
            mixer = (c_out.reshape(tokens, N_HEADS_C * C_V_DIM), d_out,
                     w_out[:N_HEADS_C * C_V_DIM], w_out[N_HEADS_C * C_V_DIM:],
                     pl.BlockSpec((tm, D_WIDTH),
                                  lambda r: (r % tiles_per_seq, r // tiles_per_seq)))
        x2 = _ffn(x2, row(ffn2_norm[layer]), *[w[layer] for w in ffn2_w], tm=tm, mixer=mixer)
    return x2.reshape(bsz, seq, d_model)
```
